```python
import math
import jax, jax.numpy as jnp
from jax import lax
import numpy as np

D_MODEL = 1024
BATCH = 1
SEQ = 16384
DEPTH = 4

N_MIXERS = 4
EPS = 1e-6

SB_HEADS = 16
SB_HEAD_DIM = D_MODEL // SB_HEADS
SB_WIDTH = SB_HEADS * SB_HEAD_DIM
SB_BLOCK = 128

HG_HEAD_DIM = 128
HG_WIDTH = D_MODEL
HG_HEADS = HG_WIDTH // HG_HEAD_DIM
HG_CHUNK = 64

M2_WIDTH = 2 * D_MODEL
M2_HEAD_DIM = 64
M2_HEADS = M2_WIDTH // M2_HEAD_DIM
M2_GROUPS = 8
M2_STATE = 128
M2_CONV = 4
M2_CHUNK = 128
M2_CONV_DIM = M2_WIDTH + 2 * M2_GROUPS * M2_STATE
M2_IN = M2_WIDTH + M2_CONV_DIM + M2_HEADS

RET_HEADS = 4
RET_QK_DIM = D_MODEL // RET_HEADS
RET_V_WIDTH = 2 * D_MODEL
RET_V_DIM = RET_V_WIDTH // RET_HEADS
RET_CHUNK = 128
RET_ROT_BASE = 10000.0
RET_IN = 2 * D_MODEL + 2 * RET_V_WIDTH

kernel_name = "hybrid_interleaved_sb_hgrn2_ssd_retention"

F32 = jnp.float32


def _rms(x, eps=EPS):
    xf = x.astype(F32)
    return xf * lax.rsqrt(jnp.mean(xf * xf, axis=-1, keepdims=True) + eps)


def rmsnorm(x, g):
    return (_rms(x) * g.astype(F32)).astype(x.dtype)


def _layers_of(m):
    return len(range(m, DEPTH, N_MIXERS))


def stick_breaking_attention(q, k, v):
    b, h, s, d = q.shape
    nb = s // SB_BLOCK
    scale = d ** -0.5
    kf = k.astype(F32)
    vf = v.astype(F32)
    key_pos = jnp.arange(s)
    q_blocks = q.reshape(b, h, nb, SB_BLOCK, d).transpose(2, 0, 1, 3, 4)

    def one_block(args):
        qb, blk = args
        q_pos = blk * SB_BLOCK + jnp.arange(SB_BLOCK)
        mask = key_pos[None, :] < q_pos[:, None]
        z = jnp.einsum('bhqd,bhkd->bhqk', qb.astype(F32), kf) * scale
        log_beta = jax.nn.log_sigmoid(z)
        log_1m = jnp.where(mask, log_beta - z, 0.0)
        after = lax.cumsum(log_1m, axis=3, reverse=True) - log_1m
        a = jnp.where(mask, jnp.exp(log_beta + after), 0.0)
        return jnp.einsum('bhqk,bhkd->bhqd', a, vf)

    out = lax.map(one_block, (q_blocks, jnp.arange(nb)))
    return out.transpose(1, 2, 0, 3, 4).reshape(b, h, s, d)


def sb_mixer(h, w_in, w_out):
    b, s, _ = h.shape
    q, k, v, gate = jnp.split(h @ w_in, 4, axis=-1)
    heads = lambda t: t.reshape(b, s, SB_HEADS, SB_HEAD_DIM).transpose(0, 2, 1, 3)
    o = stick_breaking_attention(heads(q), heads(k), heads(v))
    o = o.transpose(0, 2, 1, 3).reshape(b, s, SB_WIDTH)
    return (o * jax.nn.silu(gate.astype(F32))) @ w_out


def hgrn2_mixer(h, w_in, lb, norm_g, w_out):
    b, s, _ = h.shape
    C = HG_CHUNK
    nc = s // C
    q, f_pre, i, gate = jnp.split(h @ w_in, 4, axis=-1)
    q = jax.nn.silu(q.astype(F32))
    f = lb + (1.0 - lb) * jax.nn.sigmoid(f_pre.astype(F32))
    log_f = jnp.log(f)
    k = 1.0 - f

    def chunks(t):
        return t.astype(F32).reshape(b, nc, C, HG_HEADS, HG_HEAD_DIM).transpose(1, 0, 3, 2, 4)

    tri = jnp.tril(jnp.ones((C, C), dtype=bool))[:, :, None]

    def step(state, inp):
        qb, kb, vb, gb = inp
        cum = jnp.cumsum(gb, axis=2)
        diff = cum[:, :, :, None, :] - cum[:, :, None, :, :]
        decay = jnp.exp(jnp.where(tri, diff, -jnp.inf))
        scores = jnp.einsum('bhtd,bhsd,bhtsd->bhts', qb, kb, decay)
        o = jnp.einsum('bhts,bhsv->bhtv', scores, vb) \
            + jnp.einsum('bhtd,bhdv->bhtv', qb * jnp.exp(cum), state)
        last = cum[:, :, -1:, :]
        new_state = jnp.exp(last[:, :, 0, :, None]) * state \
            + jnp.einsum('bhsd,bhsv->bhdv', kb * jnp.exp(last - cum), vb)
        return new_state, o

    init = jnp.zeros((b, HG_HEADS, HG_HEAD_DIM, HG_HEAD_DIM), F32)
    _, o = lax.scan(step, init, (chunks(q), chunks(k), chunks(i), chunks(log_f)))
    o = o.transpose(1, 0, 3, 2, 4).reshape(b, s, HG_HEADS, HG_HEAD_DIM)
    o = _rms(o) * norm_g.reshape(HG_HEADS, HG_HEAD_DIM).astype(F32)
    o = o.reshape(b, s, HG_WIDTH)
    return (o * jax.nn.silu(gate.astype(F32))) @ w_out


def causal_depthwise_conv(x, w, bias):
    ch = x.shape[-1]
    y = lax.conv_general_dilated(
        x, w[:, None, :].astype(x.dtype), window_strides=(1,),
        padding=[(w.shape[0] - 1, 0)], dimension_numbers=('NWC', 'WIO', 'NWC'),
        feature_group_count=ch)
    return y + bias


def ssd_scan(x, a_dt, bm, cm):
    b, s, H, P = x.shape
    G, N = bm.shape[2], bm.shape[3]
    R = H // G
    L = M2_CHUNK
    nc = s // L
    x = x.astype(F32).reshape(b, nc, L, G, R, P)
    a = a_dt.reshape(b, nc, L, G, R).transpose(0, 1, 3, 4, 2)
    bm = bm.astype(F32).reshape(b, nc, L, G, N)
    cm = cm.astype(F32).reshape(b, nc, L, G, N)
    a_cum = jnp.cumsum(a, axis=-1)
    tri = jnp.tril(jnp.ones((L, L), dtype=bool))
    seg = a_cum[..., :, None] - a_cum[..., None, :]
    decay = jnp.exp(jnp.where(tri, seg, -jnp.inf))
    cb = jnp.einsum('bclgn,bcsgn->bcgls', cm, bm)
    y_diag = jnp.einsum('bcgls,bcgrls,bcsgrp->bclgrp', cb, decay, x)
    decay_to_end = jnp.exp(a_cum[..., -1:] - a_cum)
    chunk_states = jnp.einsum('bclgn,bcgrl,bclgrp->bcgrpn', bm, decay_to_end, x)
    chunk_decay = jnp.exp(a_cum[..., -1])

    def step(state, inp):
        st, dec = inp
        return dec[..., None, None] * state + st, state

    init = jnp.zeros((b, G, R, P, N), F32)
    _, prev = lax.scan(step, init, (jnp.moveaxis(chunk_states, 1, 0), jnp.moveaxis(chunk_decay, 1, 0)))
    prev = jnp.moveaxis(prev, 0, 1)
    y_off = jnp.einsum('bclgn,bcgrpn,bcgrl->bclgrp', cm, prev, jnp.exp(a_cum))
    return (y_diag + y_off).reshape(b, s, H, P)


def mamba2_mixer(h, w_in, conv_w, conv_b, dt_bias, a_log, d_skip, norm_g, w_out):
    b, s, _ = h.shape
    z, xbc, dt = jnp.split(h @ w_in, [M2_WIDTH, M2_WIDTH + M2_CONV_DIM], axis=-1)
    xbc = jax.nn.silu(causal_depthwise_conv(xbc, conv_w, conv_b))
    xs, bm, cm = jnp.split(xbc, [M2_WIDTH, M2_WIDTH + M2_GROUPS * M2_STATE], axis=-1)
    dt = jax.nn.softplus(dt.astype(F32) + dt_bias.astype(F32))
    a = -jnp.exp(a_log.astype(F32))
    xh = xs.astype(F32).reshape(b, s, M2_HEADS, M2_HEAD_DIM)
    y = ssd_scan(xh * dt[..., None], dt * a,
                 bm.reshape(b, s, M2_GROUPS, M2_STATE), cm.reshape(b, s, M2_GROUPS, M2_STATE))
    y = y + d_skip.astype(F32)[:, None] * xh
    y = y.reshape(b, s, M2_WIDTH) * jax.nn.silu(z.astype(F32))
    gs = M2_WIDTH // M2_GROUPS
    y = (_rms(y.reshape(b, s, M2_GROUPS, gs)) * norm_g.reshape(M2_GROUPS, gs).astype(F32)).reshape(b, s, M2_WIDTH)
    return y @ w_out


def theta_rotate(x):
    b, s, H, d = x.shape
    angle = 1.0 / (RET_ROT_BASE ** jnp.linspace(0.0, 1.0, d // 2, dtype=F32))
    ph = jnp.arange(s, dtype=F32)[:, None] * angle[None, :]
    cos, sin = jnp.cos(ph)[None, :, None, :], jnp.sin(ph)[None, :, None, :]
    xp = x.astype(F32).reshape(b, s, H, d // 2, 2)
    x0, x1 = xp[..., 0], xp[..., 1]
    return jnp.stack([x0 * cos - x1 * sin, x1 * cos + x0 * sin], axis=-1).reshape(b, s, H, d)


def chunk_retention(q, k, v):
    b, s, H, dk = q.shape
    dv = v.shape[-1]
    C = RET_CHUNK
    nc = s // C
    log_g = jnp.log1p(-jnp.exp2(-5.0 - jnp.arange(H, dtype=F32)))
    qc = q.reshape(b, nc, C, H, dk)
    kc = k.reshape(b, nc, C, H, dk)
    vc = v.astype(F32).reshape(b, nc, C, H, dv)
    idx = jnp.arange(C, dtype=F32)
    rel = idx[:, None] - idx[None, :]
    intra = jnp.where(rel >= 0, jnp.exp(log_g[:, None, None] * jnp.maximum(rel, 0.0)), 0.0)
    scores = jnp.einsum('bcthd,bcshd->bchts', qc, kc) * intra
    o_intra = jnp.einsum('bchts,bcshv->bcthv', scores, vc)
    k_decay = jnp.exp(log_g[:, None] * (C - 1.0 - idx))
    chunk_states = jnp.einsum('bcshd,hs,bcshv->bchdv', kc, k_decay, vc)
    chunk_decay = jnp.exp(log_g * C)[:, None, None]

    def step(state, st):
        return chunk_decay * state + st, state

    init = jnp.zeros((b, H, dk, dv), F32)
    _, prev = lax.scan(step, init, jnp.moveaxis(chunk_states, 1, 0))
    prev = jnp.moveaxis(prev, 0, 1)
    q_decay = jnp.exp(log_g[:, None] * (idx + 1.0))
    o_inter = jnp.einsum('bcthd,ht,bchdv->bcthv', qc, q_decay, prev)
    return (o_intra + o_inter).reshape(b, s, H, dv)


def retention_mixer(h, w_in, w_out):
    b, s, _ = h.shape
    q, k, v, gate = jnp.split(h @ w_in, [D_MODEL, 2 * D_MODEL, 2 * D_MODEL + RET_V_WIDTH], axis=-1)
    q = theta_rotate(q.reshape(b, s, RET_HEADS, RET_QK_DIM))
    k = theta_rotate(k.reshape(b, s, RET_HEADS, RET_QK_DIM)) * (RET_QK_DIM ** -0.5)
    o = chunk_retention(q, k, v.reshape(b, s, RET_HEADS, RET_V_DIM))
    o = _rms(o).reshape(b, s, RET_V_WIDTH)
    return (o * jax.nn.silu(gate.astype(F32))) @ w_out


def setup_inputs(seed: int = 0) -> dict:
    key = jax.random.key(seed)
    ks = jax.random.split(key, 24)
    nA, nB, nC, nD = (_layers_of(m) for m in range(N_MIXERS))

    def w(k, shape, fan_in):
        return jax.random.normal(k, shape, F32) * (fan_in ** -0.5)

    def gain(k, shape):
        return 1.0 + 0.02 * jax.random.normal(k, shape, F32)

    dt0 = jnp.exp(jax.random.uniform(ks[12], (nC, M2_HEADS), F32, math.log(1e-3), math.log(1e-1)))
    return {
        "x": jax.random.normal(ks[0], (BATCH, SEQ, D_MODEL), F32),
        "norm_g": gain(ks[1], (DEPTH, D_MODEL)),
        "sb_w_in": w(ks[2], (nA, D_MODEL, 4 * SB_WIDTH), D_MODEL),
        "sb_w_out": w(ks[3], (nA, SB_WIDTH, D_MODEL), SB_WIDTH),
        "hg_w_in": w(ks[4], (nB, D_MODEL, 4 * HG_WIDTH), D_MODEL),
        "hg_lb_logits": 0.5 * jax.random.normal(ks[5], (DEPTH, HG_WIDTH), F32),
        "hg_norm_g": gain(ks[6], (nB, HG_WIDTH)),
        "hg_w_out": w(ks[7], (nB, HG_WIDTH, D_MODEL), HG_WIDTH),
        "m2_w_in": w(ks[8], (nC, D_MODEL, M2_IN), D_MODEL),
        "m2_conv_w": w(ks[9], (nC, M2_CONV, M2_CONV_DIM), M2_CONV),
        "m2_conv_b": 0.02 * jax.random.normal(ks[10], (nC, M2_CONV_DIM), F32),
        "m2_dt_bias": dt0 + jnp.log(-jnp.expm1(-dt0)),
        "m2_a_log": jnp.log(jax.random.uniform(ks[13], (nC, M2_HEADS), F32, 1.0, 16.0)),
        "m2_d": gain(ks[14], (nC, M2_HEADS)),
        "m2_norm_g": gain(ks[15], (nC, M2_WIDTH)),
        "m2_w_out": w(ks[16], (nC, M2_WIDTH, D_MODEL), M2_WIDTH),
        "ret_w_in": w(ks[17], (nD, D_MODEL, RET_IN), D_MODEL),
        "ret_w_out": w(ks[18], (nD, RET_V_WIDTH, D_MODEL), RET_V_WIDTH),
        "final_g": gain(ks[19], (D_MODEL,)),
    }


def reference(x, norm_g, sb_w_in, sb_w_out, hg_w_in, hg_lb_logits, hg_norm_g, hg_w_out,
              m2_w_in, m2_conv_w, m2_conv_b, m2_dt_bias, m2_a_log, m2_d, m2_norm_g, m2_w_out,
              ret_w_in, ret_w_out, final_g):
    lb_cum = jnp.cumsum(jax.nn.softmax(hg_lb_logits.astype(F32), axis=0), axis=0)
    lower_bounds = lb_cum - lb_cum[0]
    for i in range(DEPTH):
        m, j = i % N_MIXERS, i // N_MIXERS
        h = rmsnorm(x, norm_g[i])
        if m == 0:
            y = sb_mixer(h, sb_w_in[j], sb_w_out[j])
        elif m == 1:
            y = hgrn2_mixer(h, hg_w_in[j], lower_bounds[i], hg_norm_g[j], hg_w_out[j])
        elif m == 2:
            y = mamba2_mixer(h, m2_w_in[j], m2_conv_w[j], m2_conv_b[j], m2_dt_bias[j],
                             m2_a_log[j], m2_d[j], m2_norm_g[j], m2_w_out[j])
        else:
            y = retention_mixer(h, ret_w_in[j], ret_w_out[j])
        x = x + y.astype(x.dtype)
    return rmsnorm(x, final_g)
```

```python
import functools
import math

import jax
import jax.numpy as jnp
from jax import lax
from jax.experimental import pallas as pl
from jax.experimental.pallas import tpu as pltpu

F32 = jnp.float32
BF16 = jnp.bfloat16
EPS = 1e-6
LANES = 128
SUBLANES = 8
VMEM_LIMIT = 48 * 1024 * 1024

D_MODEL = 1024
SB_HEADS, SB_HEAD_DIM, SB_BLOCK = 16, 64, 128
HG_HEADS, HG_HEAD_DIM, HG_CHUNK = 8, 128, 64
M2_WIDTH, M2_HEADS, M2_HEAD_DIM, M2_GROUPS, M2_STATE, M2_CONV, M2_CHUNK = 2048, 32, 64, 8, 128, 4, 128
RET_HEADS, RET_QK_DIM, RET_V_DIM, RET_V_WIDTH, RET_CHUNK = 4, 256, 512, 2048, 128
RET_ROT_BASE = 10000.0
SB_UNDERFLOW = -104.0


def _dot(a, b):
    return jnp.dot(a, b, preferred_element_type=F32)


def _dot_nt(a, b):
    return lax.dot_general(a, b, (((1,), (1,)), ((), ())), preferred_element_type=F32)


def _dot_tn(a, b):
    return lax.dot_general(a, b, (((0,), (0,)), ((), ())), preferred_element_type=F32)


def _split3(x):
    hi = x.astype(BF16)
    r1 = x - hi.astype(F32)
    mid = r1.astype(BF16)
    lo = (r1 - mid.astype(F32)).astype(BF16)
    return hi, mid, lo


def _dot_exact_right(x, m01):
    hi, mid, lo = _split3(x)
    return _dot(hi, m01) + _dot(mid, m01) + _dot(lo, m01)


def _dot_exact_left(m01, x):
    hi, mid, lo = _split3(x)
    return _dot(m01, hi) + _dot(m01, mid) + _dot(m01, lo)


def _sigmoid(x):
    return 1.0 / (1.0 + jnp.exp(-x))


def _silu(x):
    return x * _sigmoid(x)


def _softplus(x):
    return jnp.maximum(x, 0.0) + jnp.log1p(jnp.exp(-jnp.abs(x)))


def _params(*sem):
    return pltpu.CompilerParams(dimension_semantics=sem, vmem_limit_bytes=VMEM_LIMIT)


def _norm_proj_kernel(x_ref, g_ref, w_ref, o_ref, h_ref):
    @pl.when(pl.program_id(1) == 0)
    def _():
        xf = x_ref[...]
        ms = jnp.mean(xf * xf, axis=-1, keepdims=True)
        h_ref[...] = (xf * lax.rsqrt(ms + EPS) * g_ref[...]).astype(BF16)

    o_ref[...] = _dot(h_ref[...], w_ref[...]).astype(o_ref.dtype)


def _norm_proj(x, g, w, out_dtype, tn, tm=512):
    s, d = x.shape
    n = w.shape[1]
    return pl.pallas_call(
        _norm_proj_kernel,
        out_shape=jax.ShapeDtypeStruct((s, n), out_dtype),
        grid=(s // tm, n // tn),
        in_specs=[pl.BlockSpec((tm, d), lambda i, j: (i, 0)),
                  pl.BlockSpec((1, d), lambda i, j: (0, 0)),
                  pl.BlockSpec((d, tn), lambda i, j: (0, j))],
        out_specs=pl.BlockSpec((tm, tn), lambda i, j: (i, j)),
        scratch_shapes=[pltpu.VMEM((tm, d), BF16)],
        compiler_params=_params("arbitrary", "arbitrary"),
        name="norm_proj",
    )(x, g.reshape(1, d), w)


def _out_proj_kernel(og_ref, w_ref, x_ref, o_ref):
    o_ref[...] = x_ref[...] + _dot(og_ref[...], w_ref[...])


def _out_proj_final_kernel(og_ref, w_ref, x_ref, fg_ref, o_ref):
    y = x_ref[...] + _dot(og_ref[...], w_ref[...])
    ms = jnp.mean(y * y, axis=-1, keepdims=True)
    o_ref[...] = y * lax.rsqrt(ms + EPS) * fg_ref[...]


def _out_proj(og, w, x, final_g=None, tm=512):
    s, width = og.shape
    d = x.shape[1]
    in_specs = [pl.BlockSpec((tm, width), lambda i: (i, 0)),
                pl.BlockSpec((width, d), lambda i: (0, 0)),
                pl.BlockSpec((tm, d), lambda i: (i, 0))]
    args = [og, w, x]
    body = _out_proj_kernel
    if final_g is not None:
        in_specs.append(pl.BlockSpec((1, d), lambda i: (0, 0)))
        args.append(final_g.reshape(1, d))
        body = _out_proj_final_kernel
    return pl.pallas_call(
        body,
        out_shape=jax.ShapeDtypeStruct((s, d), F32),
        grid=(s // tm,),
        in_specs=in_specs,
        out_specs=pl.BlockSpec((tm, d), lambda i: (i, 0)),
        compiler_params=_params("arbitrary"),
        name="out_proj",
    )(*args)


def _sb_kernel(q_ref, k_ref, v_ref, g_ref, o_ref, acc_ref, c_ref):
    i = pl.program_id(1)
    tq = q_ref.shape[0]
    scale = SB_HEAD_DIM ** -0.5
    row = lax.broadcasted_iota(jnp.int32, (tq, SB_BLOCK), 0)
    col = lax.broadcasted_iota(jnp.int32, (tq, SB_BLOCK), 1)
    m_rev = jnp.where(row >= col, 1.0, 0.0).astype(BF16)
    q = q_ref[...]
    out = jnp.zeros((tq, LANES), F32)
    for half in range(LANES // SB_HEAD_DIM):
        hmask = (col >= half * SB_HEAD_DIM) & (col < (half + 1) * SB_HEAD_DIM)
        qh = jnp.where(hmask, q, jnp.zeros_like(q))
        acc_ref[...] = jnp.zeros_like(acc_ref)
        c_ref[...] = jnp.zeros_like(c_ref)

        def cond(st):
            j, cmax = st
            return jnp.logical_and(j >= 0, cmax > SB_UNDERFLOW)

        def body(st):
            j, _ = st
            ks = pl.multiple_of(j * SB_BLOCK, SB_BLOCK)
            kb = k_ref[pl.ds(ks, SB_BLOCK), :]
            vb = v_ref[pl.ds(ks, SB_BLOCK), :]
            z = _dot_nt(qh, kb) * scale
            mask = (j * SB_BLOCK + col) < (i * tq + row)
            l1m = jnp.where(mask, -_softplus(z), 0.0)
            r = _dot_exact_right(l1m, m_rev)
            c = c_ref[...]
            a = jnp.where(mask, jnp.exp(z + r + c), 0.0)
            acc_ref[...] += _dot(a.astype(BF16), vb)
            cn = c + jnp.sum(l1m, axis=1, keepdims=True)
            c_ref[...] = cn
            return j - 1, jnp.max(cn)

        lax.while_loop(cond, body, (i, jnp.float32(0.0)))
        out = jnp.where(hmask, acc_ref[...], out)
    gate = g_ref[...].astype(F32)
    o_ref[...] = (out * _silu(gate)).astype(o_ref.dtype)


def _sb_core(proj):
    s = proj.shape[0]
    nb = D_MODEL // LANES
    tq = SB_BLOCK
    return pl.pallas_call(
        _sb_kernel,
        out_shape=jax.ShapeDtypeStruct((s, D_MODEL), BF16),
        grid=(nb, s // tq),
        in_specs=[pl.BlockSpec((tq, LANES), lambda p, i: (i, p)),
                  pl.BlockSpec((s, LANES), lambda p, i: (0, nb + p)),
                  pl.BlockSpec((s, LANES), lambda p, i: (0, 2 * nb + p)),
                  pl.BlockSpec((tq, LANES), lambda p, i: (i, 3 * nb + p))],
        out_specs=pl.BlockSpec((tq, LANES), lambda p, i: (i, p)),
        scratch_shapes=[pltpu.VMEM((tq, LANES), F32), pltpu.VMEM((tq, LANES), F32)],
        compiler_params=_params("arbitrary", "arbitrary"),
        name="sb_attention",
    )(proj, proj, proj, proj)


def _hgrn_kernel(q_ref, f_ref, i_ref, g_ref, lb_ref, ng_ref, o_ref, st_ref, gc_ref, kc_ref, *, chunk):
    @pl.when(pl.program_id(1) == 0)
    def _():
        st_ref[...] = jnp.zeros_like(st_ref)

    c = chunk
    nblk = c // SUBLANES
    row = lax.broadcasted_iota(jnp.int32, (c, c), 0)
    col = lax.broadcasted_iota(jnp.int32, (c, c), 1)
    tri = jnp.where(row >= col, 1.0, 0.0).astype(BF16)
    lane8 = lax.broadcasted_iota(jnp.int32, (SUBLANES, c), 1)
    lb = lb_ref[...]
    ng = ng_ref[...]

    def chunk_body(ci, carry):
        r0 = pl.multiple_of(ci * c, c)
        qv = q_ref[pl.ds(r0, c), :]
        q = _silu(qv)
        f = lb + (1.0 - lb) * _sigmoid(f_ref[pl.ds(r0, c), :])
        k = 1.0 - f
        v = i_ref[pl.ds(r0, c), :]
        g = _dot_exact_left(tri, jnp.log(f))
        gc_ref[...] = g
        kc_ref[...] = k
        blocks = []
        for rb in range(nblk):
            gb = g[rb * SUBLANES:(rb + 1) * SUBLANES, :]
            qb = q[rb * SUBLANES:(rb + 1) * SUBLANES, :]
            sb = jnp.zeros((SUBLANES, c), F32)
            for s in range((rb + 1) * SUBLANES):
                e = jnp.exp(jnp.minimum(gb - gc_ref[s:s + 1, :], 0.0))
                colv = jnp.sum(qb * e * kc_ref[s:s + 1, :], axis=1, keepdims=True)
                sb = jnp.where(lane8 == s, colv, sb)
            blocks.append(sb)
        sc = jnp.where(row >= col, jnp.concatenate(blocks, axis=0), 0.0)
        vb = v.astype(BF16)
        st = st_ref[...]
        o = _dot(sc.astype(BF16), vb) + _dot_nt((q * jnp.exp(g)).astype(BF16), st.astype(BF16))
        glast = gc_ref[c - 1:c, :]
        kd = k * jnp.exp(glast - g)
        st_ref[...] = st * jnp.exp(glast) + _dot_tn(vb, kd.astype(BF16))
        ms = jnp.mean(o * o, axis=-1, keepdims=True)
        gt = g_ref[pl.ds(r0, c), :]
        o_ref[pl.ds(r0, c), :] = (o * lax.rsqrt(ms + EPS) * ng * _silu(gt)).astype(o_ref.dtype)
        return carry

    lax.fori_loop(0, q_ref.shape[0] // c, chunk_body, 0)


def _hgrn_core(proj, lb, norm_g, rows=512, chunk=HG_CHUNK):
    s = proj.shape[0]
    rows = min(rows, s)
    nh = HG_HEADS
    blk = lambda off: pl.BlockSpec((rows, LANES), lambda h, t: (t, off * nh + h))
    vec = pl.BlockSpec((1, LANES), lambda h, t: (0, h))
    return pl.pallas_call(
        functools.partial(_hgrn_kernel, chunk=chunk),
        out_shape=jax.ShapeDtypeStruct((s, D_MODEL), BF16),
        grid=(nh, s // rows),
        in_specs=[blk(0), blk(1), blk(2), blk(3), vec, vec],
        out_specs=pl.BlockSpec((rows, LANES), lambda h, t: (t, h)),
        scratch_shapes=[pltpu.VMEM((HG_HEAD_DIM, HG_HEAD_DIM), F32),
                        pltpu.VMEM((chunk, LANES), F32), pltpu.VMEM((chunk, LANES), F32)],
        compiler_params=_params("arbitrary", "arbitrary"),
        name="hgrn2",
    )(proj, proj, proj, proj, lb.reshape(1, -1), norm_g.reshape(1, -1))


def _m2_kernel(z_ref, xs_ref, bc_ref, dt_ref, cw_ref, cb_ref, dtb_ref, alog_ref, dx_ref, ng_ref, ex_ref,
               o_ref, xbuf_ref, act_ref, st_ref, cumt_ref, dtx_ref, cumx_ref, y_ref):
    L = M2_CHUNK
    w = M2_WIDTH
    slab = 512

    @pl.when(pl.program_id(0) == 0)
    def _():
        xbuf_ref[0:SUBLANES, :] = jnp.zeros((SUBLANES, 2 * w), F32)
        st_ref[...] = jnp.zeros_like(st_ref)

    xbuf_ref[SUBLANES:SUBLANES + L, 0:w] = xs_ref[...]
    xbuf_ref[SUBLANES:SUBLANES + L, w:2 * w] = bc_ref[...]
    for c0 in range(0, 2 * w, slab):
        acc = jnp.broadcast_to(cb_ref[:, c0:c0 + slab], (L, slab))
        for kk in range(M2_CONV):
            off = SUBLANES - (M2_CONV - 1) + kk
            acc = acc + cw_ref[kk:kk + 1, c0:c0 + slab] * xbuf_ref[off:off + L, c0:c0 + slab]
        act_ref[:, c0:c0 + slab] = _silu(acc)
    xbuf_ref[0:SUBLANES, :] = xbuf_ref[L:L + SUBLANES, :]

    row = lax.broadcasted_iota(jnp.int32, (L, L), 0)
    col = lax.broadcasted_iota(jnp.int32, (L, L), 1)
    tril = row >= col
    tri = jnp.where(tril, 1.0, 0.0).astype(BF16)
    dt = _softplus(dt_ref[...] + dtb_ref[...])
    adt = dt * (-jnp.exp(alog_ref[...]))
    cum = _dot_exact_left(tri, adt)
    cumt_ref[...] = cum.T
    ex = ex_ref[...]
    dtx_ref[...] = _dot_exact_right(dt, ex)
    cumx_ref[...] = _dot_exact_right(cum, ex)
    lane_half = lax.broadcasted_iota(jnp.int32, (L, LANES), 1) // M2_HEAD_DIM

    for g in range(M2_GROUPS):
        bg = act_ref[:, w + g * M2_STATE: w + (g + 1) * M2_STATE].astype(BF16)
        cg = act_ref[:, w + (M2_GROUPS + g) * M2_STATE: w + (M2_GROUPS + g + 1) * M2_STATE].astype(BF16)
        cbm = _dot_nt(cg, bg)
        heads_per_group = M2_HEADS // M2_GROUPS
        for pp in range(heads_per_group // 2):
            pair = g * (heads_per_group // 2) + pp
            sl = slice(pair * LANES, (pair + 1) * LANES)
            xs_p = act_ref[:, sl]
            xdt = xs_p * dtx_ref[:, sl]
            xdt_b = xdt.astype(BF16)
            cumx_p = cumx_ref[:, sl]
            cum_last = cumx_ref[L - 1:L, sl]
            yd = jnp.zeros((L, LANES), F32)
            for hh in range(2):
                h = 2 * pair + hh
                cc = cumx_ref[:, h * M2_HEAD_DIM:h * M2_HEAD_DIM + 1]
                cr = cumt_ref[h:h + 1, :]
                dec = jnp.exp(jnp.where(tril, cc - cr, -jnp.inf))
                yh = _dot((cbm * dec).astype(BF16), xdt_b)
                yd = jnp.where(lane_half == hh, yh, yd)
            st = st_ref[pair]
            yoff = _dot(cg, st.astype(BF16)) * jnp.exp(cumx_p)
            dte = jnp.exp(cum_last - cumx_p)
            st_ref[pair] = st * jnp.exp(cum_last) + _dot_tn(bg, (xdt * dte).astype(BF16))
            y = yd + yoff + dx_ref[:, sl] * xs_p
            y_ref[:, sl] = y * _silu(z_ref[:, sl])
        gs = w // M2_GROUPS
        yg = y_ref[:, g * gs:(g + 1) * gs]
        ms = jnp.mean(yg * yg, axis=-1, keepdims=True)
        o_ref[:, g * gs:(g + 1) * gs] = (yg * lax.rsqrt(ms + EPS) * ng_ref[:, g * gs:(g + 1) * gs]).astype(o_ref.dtype)


def _m2_core(proj, conv_w, conv_b, dt_bias, a_log, d_skip, norm_g):
    s = proj.shape[0]
    L, w = M2_CHUNK, M2_WIDTH
    pad = LANES - M2_HEADS
    dtb = jnp.pad(dt_bias, (0, pad)).reshape(1, LANES)
    alog = jnp.pad(a_log, (0, pad)).reshape(1, LANES)
    dx = jnp.repeat(d_skip, M2_HEAD_DIM).reshape(1, w)
    ex = (jnp.arange(w)[None, :] // M2_HEAD_DIM == jnp.arange(LANES)[:, None]).astype(BF16)
    full = lambda shape: pl.BlockSpec(shape, lambda c: (0,) * len(shape))
    return pl.pallas_call(
        _m2_kernel,
        out_shape=jax.ShapeDtypeStruct((s, w), BF16),
        grid=(s // L,),
        in_specs=[pl.BlockSpec((L, w), lambda c: (c, 0)),
                  pl.BlockSpec((L, w), lambda c: (c, 1)),
                  pl.BlockSpec((L, w), lambda c: (c, 2)),
                  pl.BlockSpec((L, LANES), lambda c: (c, 3 * w // LANES)),
                  full((M2_CONV, 2 * w)), full((1, 2 * w)), full((1, LANES)), full((1, LANES)),
                  full((1, w)), full((1, w)), full((LANES, w))],
        out_specs=pl.BlockSpec((L, w), lambda c: (c, 0)),
        scratch_shapes=[pltpu.VMEM((L + SUBLANES, 2 * w), F32),
                        pltpu.VMEM((L, 2 * w), F32),
                        pltpu.VMEM((M2_HEADS // 2, M2_STATE, LANES), F32),
                        pltpu.VMEM((LANES, L), F32),
                        pltpu.VMEM((L, w), F32), pltpu.VMEM((L, w), F32), pltpu.VMEM((L, w), F32)],
        compiler_params=_params("arbitrary"),
        name="mamba2_ssd",
    )(proj, proj, proj, proj, conv_w, conv_b.reshape(1, -1), dtb, alog, dx, norm_g.reshape(1, -1), ex)


def _ret_kernel(q_ref, k_ref, v_ref, g_ref, ang_ref, lg_ref, o_ref, st_ref):
    c = pl.program_id(1)
    L = RET_CHUNK
    half = RET_QK_DIM // 2

    @pl.when(c == 0)
    def _():
        st_ref[...] = jnp.zeros_like(st_ref)

    pos = (c * L + lax.broadcasted_iota(jnp.int32, (L, half), 0)).astype(F32)
    ph = pos * ang_ref[...]
    cos, sin = jnp.cos(ph), jnp.sin(ph)

    def rot(x):
        x0, x1 = x[:, :half], x[:, half:]
        return jnp.concatenate([x0 * cos - x1 * sin, x1 * cos + x0 * sin], axis=1)

    q = rot(q_ref[...])
    k = rot(k_ref[...]) * (RET_QK_DIM ** -0.5)
    vb = v_ref[...].astype(BF16)
    lg = lg_ref[0]
    row = lax.broadcasted_iota(jnp.int32, (L, L), 0)
    col = lax.broadcasted_iota(jnp.int32, (L, L), 1)
    rel = (row - col).astype(F32)
    intra = jnp.where(rel >= 0, jnp.exp(lg * jnp.maximum(rel, 0.0)), 0.0)
    rowf = row.astype(F32)
    q_decay = jnp.exp(lg * (rowf + 1.0))
    k_decay = jnp.exp(lg * (L - 1.0 - rowf))
    qb = q.astype(BF16)
    scores = _dot_nt(qb, k.astype(BF16)) * intra
    st = st_ref[...]
    o_inter = _dot(qb, st.astype(BF16))
    o = _dot(scores.astype(BF16), vb) + jnp.concatenate([q_decay] * (RET_V_DIM // LANES), axis=1) * o_inter
    kd = k * jnp.concatenate([k_decay] * (RET_QK_DIM // LANES), axis=1)
    st_ref[...] = st * jnp.exp(lg[:, 0:1] * float(L)) + _dot_tn(kd.astype(BF16), vb)
    ms = jnp.mean(o * o, axis=-1, keepdims=True)
    o_ref[...] = (o * lax.rsqrt(ms + EPS) * _silu(g_ref[...])).astype(o_ref.dtype)


def _ret_core(proj):
    s = proj.shape[0]
    L = RET_CHUNK
    nh = RET_HEADS
    half = RET_QK_DIM // 2
    angle = (1.0 / (RET_ROT_BASE ** jnp.linspace(0.0, 1.0, half, dtype=F32))).reshape(1, half)
    log_g = jnp.log1p(-jnp.exp2(-5.0 - jnp.arange(nh, dtype=F32)))
    lg = jnp.broadcast_to(log_g[:, None, None], (nh, 1, LANES))
    vblk0 = 2 * D_MODEL // RET_V_DIM
    return pl.pallas_call(
        _ret_kernel,
        out_shape=jax.ShapeDtypeStruct((s, RET_V_WIDTH), BF16),
        grid=(nh, s // L),
        in_specs=[pl.BlockSpec((L, RET_QK_DIM), lambda h, c: (c, h)),
                  pl.BlockSpec((L, RET_QK_DIM), lambda h, c: (c, nh + h)),
                  pl.BlockSpec((L, RET_V_DIM), lambda h, c: (c, vblk0 + h)),
                  pl.BlockSpec((L, RET_V_DIM), lambda h, c: (c, vblk0 + nh + h)),
                  pl.BlockSpec((1, half), lambda h, c: (0, 0)),
                  pl.BlockSpec((1, 1, LANES), lambda h, c: (h, 0, 0))],
        out_specs=pl.BlockSpec((L, RET_V_DIM), lambda h, c: (c, h)),
        scratch_shapes=[pltpu.VMEM((RET_QK_DIM, RET_V_DIM), F32)],
        compiler_params=_params("arbitrary", "arbitrary"),
        name="retention",
    )(proj, proj, proj, proj, angle, lg)


def _ret_permute_qk(w):
    d = w.shape[0]
    qk = w[:, :2 * D_MODEL].reshape(d, 2 * RET_HEADS, RET_QK_DIM // 2, 2)
    qk = jnp.swapaxes(qk, 2, 3).reshape(d, 2 * D_MODEL)
    return jnp.concatenate([qk, w[:, 2 * D_MODEL:]], axis=1)


def kernel(x, norm_g, sb_w_in, sb_w_out, hg_w_in, hg_lb_logits, hg_norm_g, hg_w_out, m2_w_in, m2_conv_w,
           m2_conv_b, m2_dt_bias, m2_a_log, m2_d, m2_norm_g, m2_w_out, ret_w_in, ret_w_out, final_g):
    b, s, d = x.shape
    depth = norm_g.shape[0]
    lb_cum = jnp.cumsum(jax.nn.softmax(hg_lb_logits.astype(F32), axis=0), axis=0)
    lower_bounds = lb_cum - lb_cum[0]
    outs = []
    for bi in range(b):
        xb = x[bi]
        for i in range(depth):
            m, j = i % 4, i // 4
            fg = final_g if i == depth - 1 else None
            if m == 0:
                proj = _norm_proj(xb, norm_g[i], sb_w_in[j].astype(BF16), BF16, tn=1024)
                xb = _out_proj(_sb_core(proj), sb_w_out[j].astype(BF16), xb, fg)
            elif m == 1:
                proj = _norm_proj(xb, norm_g[i], hg_w_in[j].astype(BF16), F32, tn=1024)
                og = _hgrn_core(proj, lower_bounds[i], hg_norm_g[j])
                xb = _out_proj(og, hg_w_out[j].astype(BF16), xb, fg)
            elif m == 2:
                w_in = jnp.pad(m2_w_in[j], ((0, 0), (0, LANES - M2_HEADS))).astype(BF16)
                proj = _norm_proj(xb, norm_g[i], w_in, F32, tn=896)
                og = _m2_core(proj, m2_conv_w[j], m2_conv_b[j], m2_dt_bias[j], m2_a_log[j], m2_d[j],
                              m2_norm_g[j])
                xb = _out_proj(og, m2_w_out[j].astype(BF16), xb, fg)
            else:
                w_in = _ret_permute_qk(ret_w_in[j]).astype(BF16)
                proj = _norm_proj(xb, norm_g[i], w_in, F32, tn=1024)
                xb = _out_proj(_ret_core(proj), ret_w_out[j].astype(BF16), xb, fg)
        outs.append(xb)
    return jnp.stack(outs, axis=0)
```

```python
import functools
import math

import jax
import jax.numpy as jnp
from jax import lax
from jax.experimental import pallas as pl
from jax.experimental.pallas import tpu as pltpu

F32 = jnp.float32
BF16 = jnp.bfloat16
EPS = 1e-6
LANES = 128
SUBLANES = 8
VMEM_LIMIT = 48 * 1024 * 1024

D_MODEL = 1024
SB_HEADS, SB_HEAD_DIM, SB_BLOCK = 16, 64, 128
HG_HEADS, HG_HEAD_DIM, HG_CHUNK = 8, 128, 64
M2_WIDTH, M2_HEADS, M2_HEAD_DIM, M2_GROUPS, M2_STATE, M2_CONV, M2_CHUNK = 2048, 32, 64, 8, 128, 4, 128
RET_HEADS, RET_QK_DIM, RET_V_DIM, RET_V_WIDTH, RET_CHUNK = 4, 256, 512, 2048, 128
RET_ROT_BASE = 10000.0
SB_UNDERFLOW = -104.0
SB_WINDOW = 3


def _dot(a, b):
    return jnp.dot(a, b, preferred_element_type=F32)


def _dot_nt(a, b):
    return lax.dot_general(a, b, (((1,), (1,)), ((), ())), preferred_element_type=F32)


def _dot_tn(a, b):
    return lax.dot_general(a, b, (((0,), (0,)), ((), ())), preferred_element_type=F32)


def _split3(x):
    hi = x.astype(BF16)
    r1 = x - hi.astype(F32)
    mid = r1.astype(BF16)
    lo = (r1 - mid.astype(F32)).astype(BF16)
    return hi, mid, lo


def _dot_exact_right(x, m01):
    hi, mid, lo = _split3(x)
    return _dot(hi, m01) + _dot(mid, m01) + _dot(lo, m01)


def _dot_exact_left(m01, x):
    hi, mid, lo = _split3(x)
    return _dot(m01, hi) + _dot(m01, mid) + _dot(m01, lo)


def _sigmoid(x):
    return 1.0 / (1.0 + jnp.exp(-x))


def _silu(x):
    return x * _sigmoid(x)


def _softplus(x):
    return jnp.maximum(x, 0.0) + jnp.log1p(jnp.exp(-jnp.abs(x)))


def _params(*sem):
    return pltpu.CompilerParams(dimension_semantics=sem, vmem_limit_bytes=VMEM_LIMIT)


def _norm_proj_kernel(x_ref, g_ref, w_ref, o_ref, h_ref):
    @pl.when(pl.program_id(1) == 0)
    def _():
        xf = x_ref[...]
        ms = jnp.mean(xf * xf, axis=-1, keepdims=True)
        h_ref[...] = (xf * lax.rsqrt(ms + EPS) * g_ref[...]).astype(BF16)

    o_ref[...] = _dot(h_ref[...], w_ref[...]).astype(o_ref.dtype)


def _norm_proj(x, g, w, out_dtype, tn, tm=512):
    s, d = x.shape
    n = w.shape[1]
    return pl.pallas_call(
        _norm_proj_kernel,
        out_shape=jax.ShapeDtypeStruct((s, n), out_dtype),
        grid=(s // tm, n // tn),
        in_specs=[pl.BlockSpec((tm, d), lambda i, j: (i, 0)),
                  pl.BlockSpec((1, d), lambda i, j: (0, 0)),
                  pl.BlockSpec((d, tn), lambda i, j: (0, j))],
        out_specs=pl.BlockSpec((tm, tn), lambda i, j: (i, j)),
        scratch_shapes=[pltpu.VMEM((tm, d), BF16)],
        compiler_params=_params("arbitrary", "arbitrary"),
        name="norm_proj",
    )(x, g.reshape(1, d), w)


def _out_proj_kernel(og_ref, w_ref, x_ref, o_ref):
    o_ref[...] = x_ref[...] + _dot(og_ref[...], w_ref[...])


def _out_proj_final_kernel(og_ref, w_ref, x_ref, fg_ref, o_ref):
    y = x_ref[...] + _dot(og_ref[...], w_ref[...])
    ms = jnp.mean(y * y, axis=-1, keepdims=True)
    o_ref[...] = y * lax.rsqrt(ms + EPS) * fg_ref[...]


def _out_proj(og, w, x, final_g=None, tm=512):
    s, width = og.shape
    d = x.shape[1]
    in_specs = [pl.BlockSpec((tm, width), lambda i: (i, 0)),
                pl.BlockSpec((width, d), lambda i: (0, 0)),
                pl.BlockSpec((tm, d), lambda i: (i, 0))]
    args = [og, w, x]
    body = _out_proj_kernel
    if final_g is not None:
        in_specs.append(pl.BlockSpec((1, d), lambda i: (0, 0)))
        args.append(final_g.reshape(1, d))
        body = _out_proj_final_kernel
    return pl.pallas_call(
        body,
        out_shape=jax.ShapeDtypeStruct((s, d), F32),
        grid=(s // tm,),
        in_specs=in_specs,
        out_specs=pl.BlockSpec((tm, d), lambda i: (i, 0)),
        compiler_params=_params("arbitrary"),
        name="out_proj",
    )(*args)


def _sb_span(qs, k_ref, v_ref, kblk0, nblk, q0, c, m_rev):
    rows = qs.shape[0]
    tq = rows // (LANES // SB_HEAD_DIM)
    width = nblk * SB_BLOCK
    ks = pl.multiple_of(kblk0 * SB_BLOCK, SB_BLOCK)
    z = _dot_nt(qs, k_ref[pl.ds(ks, width), :])
    row = lax.broadcasted_iota(jnp.int32, (rows, width), 0)
    col = lax.broadcasted_iota(jnp.int32, (rows, width), 1)
    mask = (ks + col) < (q0 + (row & (tq - 1)))
    l1m = jnp.where(mask, -_softplus(z), 0.0)
    blk = lambda x, b: x[:, b * SB_BLOCK:(b + 1) * SB_BLOCK]
    stacked = jnp.concatenate([blk(l1m, b) for b in range(nblk)], axis=0)
    rev = _dot_exact_right(stacked, m_rev)
    parts = [None] * nblk
    for b in reversed(range(nblk)):
        e = jnp.exp(blk(z, b) + rev[b * rows:(b + 1) * rows] + c)
        parts[b] = jnp.where(blk(mask, b), e, 0.0).astype(BF16)
        c = c + jnp.sum(blk(l1m, b), axis=1, keepdims=True)
    a = jnp.concatenate(parts, axis=1)
    return _dot(a, v_ref[pl.ds(ks, width), :]), c


def _sb_kernel(q_ref, k_ref, v_ref, g_ref, o_ref, acc_ref, c_ref):
    i = pl.program_id(1)
    tq = q_ref.shape[0]
    assert tq & (tq - 1) == 0
    nkb = k_ref.shape[0] // SB_BLOCK
    n_half = LANES // SB_HEAD_DIM
    row = lax.broadcasted_iota(jnp.int32, (SB_BLOCK, SB_BLOCK), 0)
    col = lax.broadcasted_iota(jnp.int32, (SB_BLOCK, SB_BLOCK), 1)
    m_rev = jnp.where(row >= col, 1.0, 0.0).astype(BF16)
    q = q_ref[...] * (SB_HEAD_DIM ** -0.5)
    lane = lax.broadcasted_iota(jnp.int32, (tq, LANES), 1)
    hmasks = [(lane >= h * SB_HEAD_DIM) & (lane < (h + 1) * SB_HEAD_DIM) for h in range(n_half)]
    qs = jnp.concatenate([jnp.where(m, q, jnp.zeros_like(q)) for m in hmasks], axis=0)
    q0 = i * tq
    w0 = jnp.clip(i - (SB_WINDOW - 1), 0, nkb - SB_WINDOW)
    pv, c = _sb_span(qs, k_ref, v_ref, w0, SB_WINDOW, q0, jnp.zeros((n_half * tq, 1), F32), m_rev)
    acc_ref[...] = pv
    c_ref[...] = jnp.broadcast_to(c, c_ref.shape)

    def cond(st):
        j, cmax = st
        return jnp.logical_and(j >= 0, cmax > SB_UNDERFLOW)

    def body(st):
        j, _ = st
        pv, cn = _sb_span(qs, k_ref, v_ref, j, 1, q0, c_ref[...], m_rev)
        acc_ref[...] += pv
        c_ref[...] = cn
        return j - 1, jnp.max(cn)

    lax.while_loop(cond, body, (w0 - 1, jnp.max(c)))
    out = acc_ref[0:tq, :]
    for h in range(1, n_half):
        out = jnp.where(hmasks[h], acc_ref[h * tq:(h + 1) * tq, :], out)
    gate = g_ref[...].astype(F32)
    o_ref[...] = (out * _silu(gate)).astype(o_ref.dtype)


def _sb_core(proj):
    s = proj.shape[0]
    nb = D_MODEL // LANES
    tq = SB_BLOCK
    return pl.pallas_call(
        _sb_kernel,
        out_shape=jax.ShapeDtypeStruct((s, D_MODEL), BF16),
        grid=(nb, s // tq),
        in_specs=[pl.BlockSpec((tq, LANES), lambda p, i: (i, p)),
                  pl.BlockSpec((s, LANES), lambda p, i: (0, nb + p)),
                  pl.BlockSpec((s, LANES), lambda p, i: (0, 2 * nb + p)),
                  pl.BlockSpec((tq, LANES), lambda p, i: (i, 3 * nb + p))],
        out_specs=pl.BlockSpec((tq, LANES), lambda p, i: (i, p)),
        scratch_shapes=[pltpu.VMEM((LANES // SB_HEAD_DIM * tq, LANES), F32)] * 2,
        compiler_params=_params("arbitrary", "arbitrary"),
        name="sb_attention",
    )(proj, proj, proj, proj)


def _hgrn_kernel(q_ref, f_ref, i_ref, g_ref, lb_ref, ng_ref, o_ref, st_ref, gc_ref, kc_ref, *, chunk):
    @pl.when(pl.program_id(1) == 0)
    def _():
        st_ref[...] = jnp.zeros_like(st_ref)

    c = chunk
    nblk = c // SUBLANES
    row = lax.broadcasted_iota(jnp.int32, (c, c), 0)
    col = lax.broadcasted_iota(jnp.int32, (c, c), 1)
    tri = jnp.where(row >= col, 1.0, 0.0).astype(BF16)
    lane8 = lax.broadcasted_iota(jnp.int32, (SUBLANES, c), 1)
    lb = lb_ref[...]
    ng = ng_ref[...]

    def chunk_body(ci, carry):
        r0 = pl.multiple_of(ci * c, c)
        qv = q_ref[pl.ds(r0, c), :]
        q = _silu(qv)
        f = lb + (1.0 - lb) * _sigmoid(f_ref[pl.ds(r0, c), :])
        k = 1.0 - f
        v = i_ref[pl.ds(r0, c), :]
        g = _dot_exact_left(tri, jnp.log(f))
        gc_ref[...] = g
        kc_ref[...] = k
        blocks = []
        for rb in range(nblk):
            gb = g[rb * SUBLANES:(rb + 1) * SUBLANES, :]
            qb = q[rb * SUBLANES:(rb + 1) * SUBLANES, :]
            sb = jnp.zeros((SUBLANES, c), F32)
            for s in range((rb + 1) * SUBLANES):
                e = jnp.exp(jnp.minimum(gb - gc_ref[s:s + 1, :], 0.0))
                colv = jnp.sum(qb * e * kc_ref[s:s + 1, :], axis=1, keepdims=True)
                sb = jnp.where(lane8 == s, colv, sb)
            blocks.append(sb)
        sc = jnp.where(row >= col, jnp.concatenate(blocks, axis=0), 0.0)
        vb = v.astype(BF16)
        st = st_ref[...]
        o = _dot(sc.astype(BF16), vb) + _dot_nt((q * jnp.exp(g)).astype(BF16), st.astype(BF16))
        glast = gc_ref[c - 1:c, :]
        kd = k * jnp.exp(glast - g)
        st_ref[...] = st * jnp.exp(glast) + _dot_tn(vb, kd.astype(BF16))
        ms = jnp.mean(o * o, axis=-1, keepdims=True)
        gt = g_ref[pl.ds(r0, c), :]
        o_ref[pl.ds(r0, c), :] = (o * lax.rsqrt(ms + EPS) * ng * _silu(gt)).astype(o_ref.dtype)
        return carry

    lax.fori_loop(0, q_ref.shape[0] // c, chunk_body, 0)


def _hgrn_core(proj, lb, norm_g, rows=512, chunk=HG_CHUNK):
    s = proj.shape[0]
    rows = min(rows, s)
    nh = HG_HEADS
    blk = lambda off: pl.BlockSpec((rows, LANES), lambda h, t: (t, off * nh + h))
    vec = pl.BlockSpec((1, LANES), lambda h, t: (0, h))
    return pl.pallas_call(
        functools.partial(_hgrn_kernel, chunk=chunk),
        out_shape=jax.ShapeDtypeStruct((s, D_MODEL), BF16),
        grid=(nh, s // rows),
        in_specs=[blk(0), blk(1), blk(2), blk(3), vec, vec],
        out_specs=pl.BlockSpec((rows, LANES), lambda h, t: (t, h)),
        scratch_shapes=[pltpu.VMEM((HG_HEAD_DIM, HG_HEAD_DIM), F32),
                        pltpu.VMEM((chunk, LANES), F32), pltpu.VMEM((chunk, LANES), F32)],
        compiler_params=_params("arbitrary", "arbitrary"),
        name="hgrn2",
    )(proj, proj, proj, proj, lb.reshape(1, -1), norm_g.reshape(1, -1))


def _m2_kernel(z_ref, xs_ref, bc_ref, dt_ref, cw_ref, cb_ref, dtb_ref, alog_ref, dx_ref, ng_ref, ex_ref,
               o_ref, xbuf_ref, act_ref, st_ref, cumt_ref, dtx_ref, cumx_ref, y_ref):
    L = M2_CHUNK
    w = M2_WIDTH
    slab = 512

    @pl.when(pl.program_id(0) == 0)
    def _():
        xbuf_ref[0:SUBLANES, :] = jnp.zeros((SUBLANES, 2 * w), F32)
        st_ref[...] = jnp.zeros_like(st_ref)

    xbuf_ref[SUBLANES:SUBLANES + L, 0:w] = xs_ref[...]
    xbuf_ref[SUBLANES:SUBLANES + L, w:2 * w] = bc_ref[...]
    for c0 in range(0, 2 * w, slab):
        acc = jnp.broadcast_to(cb_ref[:, c0:c0 + slab], (L, slab))
        for kk in range(M2_CONV):
            off = SUBLANES - (M2_CONV - 1) + kk
            acc = acc + cw_ref[kk:kk + 1, c0:c0 + slab] * xbuf_ref[off:off + L, c0:c0 + slab]
        act_ref[:, c0:c0 + slab] = _silu(acc)
    xbuf_ref[0:SUBLANES, :] = xbuf_ref[L:L + SUBLANES, :]

    row = lax.broadcasted_iota(jnp.int32, (L, L), 0)
    col = lax.broadcasted_iota(jnp.int32, (L, L), 1)
    tril = row >= col
    tri = jnp.where(tril, 1.0, 0.0).astype(BF16)
    dt = _softplus(dt_ref[...] + dtb_ref[...])
    adt = dt * (-jnp.exp(alog_ref[...]))
    cum = _dot_exact_left(tri, adt)
    cumt_ref[...] = cum.T
    ex = ex_ref[...]
    dtx_ref[...] = _dot_exact_right(dt, ex)
    cumx_ref[...] = _dot_exact_right(cum, ex)
    lane_half = lax.broadcasted_iota(jnp.int32, (L, LANES), 1) // M2_HEAD_DIM

    for g in range(M2_GROUPS):
        bg = act_ref[:, w + g * M2_STATE: w + (g + 1) * M2_STATE].astype(BF16)
        cg = act_ref[:, w + (M2_GROUPS + g) * M2_STATE: w + (M2_GROUPS + g + 1) * M2_STATE].astype(BF16)
        cbm = _dot_nt(cg, bg)
        heads_per_group = M2_HEADS // M2_GROUPS
        for pp in range(heads_per_group // 2):
            pair = g * (heads_per_group // 2) + pp
            sl = slice(pair * LANES, (pair + 1) * LANES)
            xs_p = act_ref[:, sl]
            xdt = xs_p * dtx_ref[:, sl]
            xdt_b = xdt.astype(BF16)
            cumx_p = cumx_ref[:, sl]
            cum_last = cumx_ref[L - 1:L, sl]
            yd = jnp.zeros((L, LANES), F32)
            for hh in range(2):
                h = 2 * pair + hh
                cc = cumx_ref[:, h * M2_HEAD_DIM:h * M2_HEAD_DIM + 1]
                cr = cumt_ref[h:h + 1, :]
                dec = jnp.exp(jnp.where(tril, cc - cr, -jnp.inf))
                yh = _dot((cbm * dec).astype(BF16), xdt_b)
                yd = jnp.where(lane_half == hh, yh, yd)
            st = st_ref[pair]
            yoff = _dot(cg, st.astype(BF16)) * jnp.exp(cumx_p)
            dte = jnp.exp(cum_last - cumx_p)
            st_ref[pair] = st * jnp.exp(cum_last) + _dot_tn(bg, (xdt * dte).astype(BF16))
            y = yd + yoff + dx_ref[:, sl] * xs_p
            y_ref[:, sl] = y * _silu(z_ref[:, sl])
        gs = w // M2_GROUPS
        yg = y_ref[:, g * gs:(g + 1) * gs]
        ms = jnp.mean(yg * yg, axis=-1, keepdims=True)
        o_ref[:, g * gs:(g + 1) * gs] = (yg * lax.rsqrt(ms + EPS) * ng_ref[:, g * gs:(g + 1) * gs]).astype(o_ref.dtype)


def _m2_core(proj, conv_w, conv_b, dt_bias, a_log, d_skip, norm_g):
    s = proj.shape[0]
    L, w = M2_CHUNK, M2_WIDTH
    pad = LANES - M2_HEADS
    dtb = jnp.pad(dt_bias, (0, pad)).reshape(1, LANES)
    alog = jnp.pad(a_log, (0, pad)).reshape(1, LANES)
    dx = jnp.repeat(d_skip, M2_HEAD_DIM).reshape(1, w)
    ex = (jnp.arange(w)[None, :] // M2_HEAD_DIM == jnp.arange(LANES)[:, None]).astype(BF16)
    full = lambda shape: pl.BlockSpec(shape, lambda c: (0,) * len(shape))
    return pl.pallas_call(
        _m2_kernel,
        out_shape=jax.ShapeDtypeStruct((s, w), BF16),
        grid=(s // L,),
        in_specs=[pl.BlockSpec((L, w), lambda c: (c, 0)),
                  pl.BlockSpec((L, w), lambda c: (c, 1)),
                  pl.BlockSpec((L, w), lambda c: (c, 2)),
                  pl.BlockSpec((L, LANES), lambda c: (c, 3 * w // LANES)),
                  full((M2_CONV, 2 * w)), full((1, 2 * w)), full((1, LANES)), full((1, LANES)),
                  full((1, w)), full((1, w)), full((LANES, w))],
        out_specs=pl.BlockSpec((L, w), lambda c: (c, 0)),
        scratch_shapes=[pltpu.VMEM((L + SUBLANES, 2 * w), F32),
                        pltpu.VMEM((L, 2 * w), F32),
                        pltpu.VMEM((M2_HEADS // 2, M2_STATE, LANES), F32),
                        pltpu.VMEM((LANES, L), F32),
                        pltpu.VMEM((L, w), F32), pltpu.VMEM((L, w), F32), pltpu.VMEM((L, w), F32)],
        compiler_params=_params("arbitrary"),
        name="mamba2_ssd",
    )(proj, proj, proj, proj, conv_w, conv_b.reshape(1, -1), dtb, alog, dx, norm_g.reshape(1, -1), ex)


def _ret_kernel(q_ref, k_ref, v_ref, g_ref, ang_ref, lg_ref, o_ref, st_ref):
    c = pl.program_id(1)
    L = RET_CHUNK
    half = RET_QK_DIM // 2

    @pl.when(c == 0)
    def _():
        st_ref[...] = jnp.zeros_like(st_ref)

    pos = (c * L + lax.broadcasted_iota(jnp.int32, (L, half), 0)).astype(F32)
    ph = pos * ang_ref[...]
    cos, sin = jnp.cos(ph), jnp.sin(ph)

    def rot(x):
        x0, x1 = x[:, :half], x[:, half:]
        return jnp.concatenate([x0 * cos - x1 * sin, x1 * cos + x0 * sin], axis=1)

    q = rot(q_ref[...])
    k = rot(k_ref[...]) * (RET_QK_DIM ** -0.5)
    vb = v_ref[...].astype(BF16)
    lg = lg_ref[0]
    row = lax.broadcasted_iota(jnp.int32, (L, L), 0)
    col = lax.broadcasted_iota(jnp.int32, (L, L), 1)
    rel = (row - col).astype(F32)
    intra = jnp.where(rel >= 0, jnp.exp(lg * jnp.maximum(rel, 0.0)), 0.0)
    rowf = row.astype(F32)
    q_decay = jnp.exp(lg * (rowf + 1.0))
    k_decay = jnp.exp(lg * (L - 1.0 - rowf))
    qb = q.astype(BF16)
    scores = _dot_nt(qb, k.astype(BF16)) * intra
    st = st_ref[...]
    o_inter = _dot(qb, st.astype(BF16))
    o = _dot(scores.astype(BF16), vb) + jnp.concatenate([q_decay] * (RET_V_DIM // LANES), axis=1) * o_inter
    kd = k * jnp.concatenate([k_decay] * (RET_QK_DIM // LANES), axis=1)
    st_ref[...] = st * jnp.exp(lg[:, 0:1] * float(L)) + _dot_tn(kd.astype(BF16), vb)
    ms = jnp.mean(o * o, axis=-1, keepdims=True)
    o_ref[...] = (o * lax.rsqrt(ms + EPS) * _silu(g_ref[...])).astype(o_ref.dtype)


def _ret_core(proj):
    s = proj.shape[0]
    L = RET_CHUNK
    nh = RET_HEADS
    half = RET_QK_DIM // 2
    angle = (1.0 / (RET_ROT_BASE ** jnp.linspace(0.0, 1.0, half, dtype=F32))).reshape(1, half)
    log_g = jnp.log1p(-jnp.exp2(-5.0 - jnp.arange(nh, dtype=F32)))
    lg = jnp.broadcast_to(log_g[:, None, None], (nh, 1, LANES))
    vblk0 = 2 * D_MODEL // RET_V_DIM
    return pl.pallas_call(
        _ret_kernel,
        out_shape=jax.ShapeDtypeStruct((s, RET_V_WIDTH), BF16),
        grid=(nh, s // L),
        in_specs=[pl.BlockSpec((L, RET_QK_DIM), lambda h, c: (c, h)),
                  pl.BlockSpec((L, RET_QK_DIM), lambda h, c: (c, nh + h)),
                  pl.BlockSpec((L, RET_V_DIM), lambda h, c: (c, vblk0 + h)),
                  pl.BlockSpec((L, RET_V_DIM), lambda h, c: (c, vblk0 + nh + h)),
                  pl.BlockSpec((1, half), lambda h, c: (0, 0)),
                  pl.BlockSpec((1, 1, LANES), lambda h, c: (h, 0, 0))],
        out_specs=pl.BlockSpec((L, RET_V_DIM), lambda h, c: (c, h)),
        scratch_shapes=[pltpu.VMEM((RET_QK_DIM, RET_V_DIM), F32)],
        compiler_params=_params("arbitrary", "arbitrary"),
        name="retention",
    )(proj, proj, proj, proj, angle, lg)


def _ret_permute_qk(w):
    d = w.shape[0]
    qk = w[:, :2 * D_MODEL].reshape(d, 2 * RET_HEADS, RET_QK_DIM // 2, 2)
    qk = jnp.swapaxes(qk, 2, 3).reshape(d, 2 * D_MODEL)
    return jnp.concatenate([qk, w[:, 2 * D_MODEL:]], axis=1)


def kernel(x, norm_g, sb_w_in, sb_w_out, hg_w_in, hg_lb_logits, hg_norm_g, hg_w_out, m2_w_in, m2_conv_w,
           m2_conv_b, m2_dt_bias, m2_a_log, m2_d, m2_norm_g, m2_w_out, ret_w_in, ret_w_out, final_g):
    b, s, d = x.shape
    depth = norm_g.shape[0]
    lb_cum = jnp.cumsum(jax.nn.softmax(hg_lb_logits.astype(F32), axis=0), axis=0)
    lower_bounds = lb_cum - lb_cum[0]
    outs = []
    for bi in range(b):
        xb = x[bi]
        for i in range(depth):
            m, j = i % 4, i // 4
            fg = final_g if i == depth - 1 else None
            if m == 0:
                proj = _norm_proj(xb, norm_g[i], sb_w_in[j].astype(BF16), BF16, tn=1024)
                xb = _out_proj(_sb_core(proj), sb_w_out[j].astype(BF16), xb, fg)
            elif m == 1:
                proj = _norm_proj(xb, norm_g[i], hg_w_in[j].astype(BF16), F32, tn=1024)
                og = _hgrn_core(proj, lower_bounds[i], hg_norm_g[j])
                xb = _out_proj(og, hg_w_out[j].astype(BF16), xb, fg)
            elif m == 2:
                w_in = jnp.pad(m2_w_in[j], ((0, 0), (0, LANES - M2_HEADS))).astype(BF16)
                proj = _norm_proj(xb, norm_g[i], w_in, F32, tn=896)
                og = _m2_core(proj, m2_conv_w[j], m2_conv_b[j], m2_dt_bias[j], m2_a_log[j], m2_d[j],
                              m2_norm_g[j])
                xb = _out_proj(og, m2_w_out[j].astype(BF16), xb, fg)
            else:
                w_in = _ret_permute_qk(ret_w_in[j]).astype(BF16)
                proj = _norm_proj(xb, norm_g[i], w_in, F32, tn=1024)
                xb = _out_proj(_ret_core(proj), ret_w_out[j].astype(BF16), xb, fg)
        outs.append(xb)
    return jnp.stack(outs, axis=0)
```

```python
import functools
import math

import jax
import jax.numpy as jnp
from jax import lax
from jax.experimental import pallas as pl
from jax.experimental.pallas import tpu as pltpu

F32 = jnp.float32
BF16 = jnp.bfloat16
EPS = 1e-6
LANES = 128
SUBLANES = 8
VMEM_LIMIT = 48 * 1024 * 1024

D_MODEL = 1024
SB_HEADS, SB_HEAD_DIM, SB_BLOCK = 16, 64, 128
HG_HEADS, HG_HEAD_DIM, HG_CHUNK, HG_SUB = 8, 128, 64, 16
M2_WIDTH, M2_HEADS, M2_HEAD_DIM, M2_GROUPS, M2_STATE, M2_CONV, M2_CHUNK = 2048, 32, 64, 8, 128, 4, 128
RET_HEADS, RET_QK_DIM, RET_V_DIM, RET_V_WIDTH, RET_CHUNK = 4, 256, 512, 2048, 128
RET_ROT_BASE = 10000.0
SB_UNDERFLOW = -104.0
SB_WINDOW = 3


def _dot(a, b):
    return jnp.dot(a, b, preferred_element_type=F32)


def _dot_nt(a, b):
    return lax.dot_general(a, b, (((1,), (1,)), ((), ())), preferred_element_type=F32)


def _dot_tn(a, b):
    return lax.dot_general(a, b, (((0,), (0,)), ((), ())), preferred_element_type=F32)


def _split3(x):
    hi = x.astype(BF16)
    r1 = x - hi.astype(F32)
    mid = r1.astype(BF16)
    lo = (r1 - mid.astype(F32)).astype(BF16)
    return hi, mid, lo


def _dot_exact_right(x, m01):
    hi, mid, lo = _split3(x)
    return _dot(hi, m01) + _dot(mid, m01) + _dot(lo, m01)


def _dot_exact_left(m01, x):
    hi, mid, lo = _split3(x)
    return _dot(m01, hi) + _dot(m01, mid) + _dot(m01, lo)


def _sigmoid(x):
    return 1.0 / (1.0 + jnp.exp(-x))


def _silu(x):
    return x * _sigmoid(x)


def _softplus(x):
    return jnp.maximum(x, 0.0) + jnp.log1p(jnp.exp(-jnp.abs(x)))


def _params(*sem):
    return pltpu.CompilerParams(dimension_semantics=sem, vmem_limit_bytes=VMEM_LIMIT)


def _norm_proj_kernel(x_ref, g_ref, w_ref, o_ref, h_ref):
    @pl.when(pl.program_id(1) == 0)
    def _():
        xf = x_ref[...]
        ms = jnp.mean(xf * xf, axis=-1, keepdims=True)
        h_ref[...] = (xf * lax.rsqrt(ms + EPS) * g_ref[...]).astype(BF16)

    o_ref[...] = _dot(h_ref[...], w_ref[...]).astype(o_ref.dtype)


def _norm_proj(x, g, w, out_dtype, tn=1024, tm=1024):
    s, d = x.shape
    n = w.shape[1]
    tm = min(tm, s)
    return pl.pallas_call(
        _norm_proj_kernel,
        out_shape=jax.ShapeDtypeStruct((s, n), out_dtype),
        grid=(s // tm, n // tn),
        in_specs=[pl.BlockSpec((tm, d), lambda i, j: (i, 0)),
                  pl.BlockSpec((1, d), lambda i, j: (0, 0)),
                  pl.BlockSpec((d, tn), lambda i, j: (0, j))],
        out_specs=pl.BlockSpec((tm, tn), lambda i, j: (i, j)),
        scratch_shapes=[pltpu.VMEM((tm, d), BF16)],
        compiler_params=_params("arbitrary", "arbitrary"),
        name="norm_proj",
    )(x, g.reshape(1, d), w)


def _out_proj_kernel(og_ref, w_ref, x_ref, o_ref):
    o_ref[...] = x_ref[...] + _dot(og_ref[...], w_ref[...])


def _out_proj_final_kernel(og_ref, w_ref, x_ref, fg_ref, o_ref):
    y = x_ref[...] + _dot(og_ref[...], w_ref[...])
    ms = jnp.mean(y * y, axis=-1, keepdims=True)
    o_ref[...] = y * lax.rsqrt(ms + EPS) * fg_ref[...]


def _out_proj(og, w, x, final_g=None, tm=512):
    s, width = og.shape
    d = x.shape[1]
    in_specs = [pl.BlockSpec((tm, width), lambda i: (i, 0)),
                pl.BlockSpec((width, d), lambda i: (0, 0)),
                pl.BlockSpec((tm, d), lambda i: (i, 0))]
    args = [og, w, x]
    body = _out_proj_kernel
    if final_g is not None:
        in_specs.append(pl.BlockSpec((1, d), lambda i: (0, 0)))
        args.append(final_g.reshape(1, d))
        body = _out_proj_final_kernel
    return pl.pallas_call(
        body,
        out_shape=jax.ShapeDtypeStruct((s, d), F32),
        grid=(s // tm,),
        in_specs=in_specs,
        out_specs=pl.BlockSpec((tm, d), lambda i: (i, 0)),
        compiler_params=_params("arbitrary"),
        name="out_proj",
    )(*args)


def _sb_span(qs, k_ref, v_ref, kblk0, nblk, q0, c, m_rev):
    rows = qs.shape[0]
    tq = rows // (LANES // SB_HEAD_DIM)
    width = nblk * SB_BLOCK
    ks = pl.multiple_of(kblk0 * SB_BLOCK, SB_BLOCK)
    z = _dot_nt(qs, k_ref[pl.ds(ks, width), :])
    row = lax.broadcasted_iota(jnp.int32, (rows, width), 0)
    col = lax.broadcasted_iota(jnp.int32, (rows, width), 1)
    mask = (ks + col) < (q0 + (row & (tq - 1)))
    l1m = jnp.where(mask, -_softplus(z), 0.0)
    blk = lambda x, b: x[:, b * SB_BLOCK:(b + 1) * SB_BLOCK]
    stacked = jnp.concatenate([blk(l1m, b) for b in range(nblk)], axis=0)
    rev = _dot_exact_right(stacked, m_rev)
    parts = [None] * nblk
    for b in reversed(range(nblk)):
        e = jnp.exp(blk(z, b) + rev[b * rows:(b + 1) * rows] + c)
        parts[b] = jnp.where(blk(mask, b), e, 0.0).astype(BF16)
        c = c + jnp.sum(blk(l1m, b), axis=1, keepdims=True)
    a = jnp.concatenate(parts, axis=1)
    return _dot(a, v_ref[pl.ds(ks, width), :]), c


def _sb_kernel(q_ref, k_ref, v_ref, g_ref, o_ref, acc_ref, c_ref):
    i = pl.program_id(1)
    tq = q_ref.shape[0]
    assert tq & (tq - 1) == 0
    nkb = k_ref.shape[0] // SB_BLOCK
    n_half = LANES // SB_HEAD_DIM
    row = lax.broadcasted_iota(jnp.int32, (SB_BLOCK, SB_BLOCK), 0)
    col = lax.broadcasted_iota(jnp.int32, (SB_BLOCK, SB_BLOCK), 1)
    m_rev = jnp.where(row >= col, 1.0, 0.0).astype(BF16)
    q = q_ref[...] * (SB_HEAD_DIM ** -0.5)
    lane = lax.broadcasted_iota(jnp.int32, (tq, LANES), 1)
    hmasks = [(lane >= h * SB_HEAD_DIM) & (lane < (h + 1) * SB_HEAD_DIM) for h in range(n_half)]
    qs = jnp.concatenate([jnp.where(m, q, jnp.zeros_like(q)) for m in hmasks], axis=0)
    q0 = i * tq
    w0 = jnp.clip(i - (SB_WINDOW - 1), 0, nkb - SB_WINDOW)
    pv, c = _sb_span(qs, k_ref, v_ref, w0, SB_WINDOW, q0, jnp.zeros((n_half * tq, 1), F32), m_rev)
    acc_ref[...] = pv
    c_ref[...] = jnp.broadcast_to(c, c_ref.shape)

    def cond(st):
        j, cmax = st
        return jnp.logical_and(j >= 0, cmax > SB_UNDERFLOW)

    def body(st):
        j, _ = st
        pv, cn = _sb_span(qs, k_ref, v_ref, j, 1, q0, c_ref[...], m_rev)
        acc_ref[...] += pv
        c_ref[...] = cn
        return j - 1, jnp.max(cn)

    lax.while_loop(cond, body, (w0 - 1, jnp.max(c)))
    out = acc_ref[0:tq, :]
    for h in range(1, n_half):
        out = jnp.where(hmasks[h], acc_ref[h * tq:(h + 1) * tq, :], out)
    gate = g_ref[...].astype(F32)
    o_ref[...] = (out * _silu(gate)).astype(o_ref.dtype)


def _sb_core(proj):
    s = proj.shape[0]
    nb = D_MODEL // LANES
    tq = SB_BLOCK
    return pl.pallas_call(
        _sb_kernel,
        out_shape=jax.ShapeDtypeStruct((s, D_MODEL), BF16),
        grid=(nb, s // tq),
        in_specs=[pl.BlockSpec((tq, LANES), lambda p, i: (i, p)),
                  pl.BlockSpec((s, LANES), lambda p, i: (0, nb + p)),
                  pl.BlockSpec((s, LANES), lambda p, i: (0, 2 * nb + p)),
                  pl.BlockSpec((tq, LANES), lambda p, i: (i, 3 * nb + p))],
        out_specs=pl.BlockSpec((tq, LANES), lambda p, i: (i, p)),
        scratch_shapes=[pltpu.VMEM((LANES // SB_HEAD_DIM * tq, LANES), F32)] * 2,
        compiler_params=_params("arbitrary", "arbitrary"),
        name="sb_attention",
    )(proj, proj, proj, proj)


def _hgrn_kernel(q_ref, f_ref, i_ref, g_ref, lb_ref, ng_ref, o_ref, st_ref, gc_ref, kc_ref, *, chunk):
    @pl.when(pl.program_id(1) == 0)
    def _():
        st_ref[...] = jnp.zeros_like(st_ref)

    c = chunk
    sub = HG_SUB
    nsub = c // sub
    row = lax.broadcasted_iota(jnp.int32, (c, c), 0)
    col = lax.broadcasted_iota(jnp.int32, (c, c), 1)
    tri = jnp.where(row >= col, 1.0, 0.0).astype(BF16)
    lane8 = lax.broadcasted_iota(jnp.int32, (SUBLANES, c), 1)
    krow = lax.broadcasted_iota(jnp.int32, (c, LANES), 0)
    lb = lb_ref[...]
    ng = ng_ref[...]

    def chunk_body(ci, carry):
        r0 = pl.multiple_of(ci * c, c)
        q = _silu(q_ref[pl.ds(r0, c), :].astype(F32))
        f = lb + (1.0 - lb) * _sigmoid(f_ref[pl.ds(r0, c), :].astype(F32))
        k = 1.0 - f
        vb = i_ref[pl.ds(r0, c), :].astype(BF16)
        g = _dot_exact_left(tri, jnp.log(f))
        gc_ref[...] = g
        kc_ref[...] = k
        blocks = []
        for si in range(nsub):
            lo = si * sub
            gs, qs = g[lo:lo + sub, :], q[lo:lo + sub, :]
            if si == 0:
                cross = jnp.zeros((sub, c), F32)
            else:
                ref = gc_ref[lo - 1:lo, :]
                kin = jnp.where(krow < lo, k * jnp.exp(jnp.minimum(ref - g, 0.0)), 0.0)
                cross = _dot_nt((qs * jnp.exp(gs - ref)).astype(BF16), kin.astype(BF16))
            for rb in range(sub // SUBLANES):
                gb = gs[rb * SUBLANES:(rb + 1) * SUBLANES, :]
                qb = qs[rb * SUBLANES:(rb + 1) * SUBLANES, :]
                sb = cross[rb * SUBLANES:(rb + 1) * SUBLANES, :]
                for s in range(lo, lo + (rb + 1) * SUBLANES):
                    e = jnp.exp(jnp.minimum(gb - gc_ref[s:s + 1, :], 0.0))
                    colv = jnp.sum(qb * e * kc_ref[s:s + 1, :], axis=1, keepdims=True)
                    sb = jnp.where(lane8 == s, colv, sb)
                blocks.append(sb)
        sc = jnp.where(row >= col, jnp.concatenate(blocks, axis=0), 0.0)
        st = st_ref[...]
        o = _dot(sc.astype(BF16), vb) + _dot_nt((q * jnp.exp(g)).astype(BF16), st.astype(BF16))
        glast = gc_ref[c - 1:c, :]
        kd = k * jnp.exp(glast - g)
        st_ref[...] = st * jnp.exp(glast) + _dot_tn(vb, kd.astype(BF16))
        ms = jnp.mean(o * o, axis=-1, keepdims=True)
        gt = g_ref[pl.ds(r0, c), :].astype(F32)
        o_ref[pl.ds(r0, c), :] = (o * lax.rsqrt(ms + EPS) * ng * _silu(gt)).astype(o_ref.dtype)
        return carry

    lax.fori_loop(0, q_ref.shape[0] // c, chunk_body, 0, unroll=2)


def _hgrn_core(proj, lb, norm_g, rows=512, chunk=HG_CHUNK):
    s = proj.shape[0]
    rows = min(rows, s)
    nh = HG_HEADS
    blk = lambda off: pl.BlockSpec((rows, LANES), lambda h, t: (t, off * nh + h))
    vec = pl.BlockSpec((1, LANES), lambda h, t: (0, h))
    return pl.pallas_call(
        functools.partial(_hgrn_kernel, chunk=chunk),
        out_shape=jax.ShapeDtypeStruct((s, D_MODEL), BF16),
        grid=(nh, s // rows),
        in_specs=[blk(0), blk(1), blk(2), blk(3), vec, vec],
        out_specs=pl.BlockSpec((rows, LANES), lambda h, t: (t, h)),
        scratch_shapes=[pltpu.VMEM((HG_HEAD_DIM, HG_HEAD_DIM), F32),
                        pltpu.VMEM((chunk, LANES), F32), pltpu.VMEM((chunk, LANES), F32)],
        compiler_params=_params("arbitrary", "arbitrary"),
        name="hgrn2",
    )(proj, proj, proj, proj, lb.reshape(1, -1), norm_g.reshape(1, -1))


def _m2_kernel(z_ref, xs_ref, bc_ref, dt_ref, cw_ref, cb_ref, dtb_ref, alog_ref, dx_ref, ng_ref, ex_ref,
               o_ref, xbuf_ref, act_ref, st_ref, cumt_ref, dtx_ref, cumx_ref, y_ref):
    L = M2_CHUNK
    w = M2_WIDTH
    slab = 512

    @pl.when(pl.program_id(0) == 0)
    def _():
        xbuf_ref[0:SUBLANES, :] = jnp.zeros((SUBLANES, 2 * w), F32)
        st_ref[...] = jnp.zeros_like(st_ref)

    xbuf_ref[SUBLANES:SUBLANES + L, 0:w] = xs_ref[...].astype(F32)
    xbuf_ref[SUBLANES:SUBLANES + L, w:2 * w] = bc_ref[...].astype(F32)
    for c0 in range(0, 2 * w, slab):
        xw = xbuf_ref[:, c0:c0 + slab]
        acc = cb_ref[:, c0:c0 + slab] + cw_ref[M2_CONV - 1:M2_CONV, c0:c0 + slab] * xw[SUBLANES:, :]
        for back in range(1, M2_CONV):
            shifted = pltpu.roll(xw, back, axis=0)[SUBLANES:, :]
            acc = acc + cw_ref[M2_CONV - 1 - back:M2_CONV - back, c0:c0 + slab] * shifted
        act_ref[:, c0:c0 + slab] = _silu(acc)
    xbuf_ref[0:SUBLANES, :] = xbuf_ref[L:L + SUBLANES, :]

    row = lax.broadcasted_iota(jnp.int32, (L, L), 0)
    col = lax.broadcasted_iota(jnp.int32, (L, L), 1)
    tril = row >= col
    tri = jnp.where(tril, 1.0, 0.0).astype(BF16)
    dt = _softplus(dt_ref[...] + dtb_ref[...])
    adt = dt * (-jnp.exp(alog_ref[...]))
    cum = _dot_exact_left(tri, adt)
    cumt_ref[...] = cum.T
    ex = ex_ref[...]
    dtx_ref[...] = _dot_exact_right(dt, ex)
    cumx_ref[...] = _dot_exact_right(cum, ex)
    lane_half = lax.broadcasted_iota(jnp.int32, (L, LANES), 1) // M2_HEAD_DIM

    for g in range(M2_GROUPS):
        bg = act_ref[:, w + g * M2_STATE: w + (g + 1) * M2_STATE].astype(BF16)
        cg = act_ref[:, w + (M2_GROUPS + g) * M2_STATE: w + (M2_GROUPS + g + 1) * M2_STATE].astype(BF16)
        cbm = _dot_nt(cg, bg)
        heads_per_group = M2_HEADS // M2_GROUPS
        for pp in range(heads_per_group // 2):
            pair = g * (heads_per_group // 2) + pp
            sl = slice(pair * LANES, (pair + 1) * LANES)
            xs_p = act_ref[:, sl]
            xdt = xs_p * dtx_ref[:, sl]
            xdt_b = xdt.astype(BF16)
            cumx_p = cumx_ref[:, sl]
            cum_last = cumx_ref[L - 1:L, sl]
            yd = jnp.zeros((L, LANES), F32)
            for hh in range(2):
                h = 2 * pair + hh
                cc = cumx_ref[:, h * M2_HEAD_DIM:h * M2_HEAD_DIM + 1]
                cr = cumt_ref[h:h + 1, :]
                dec = jnp.exp(jnp.where(tril, cc - cr, -jnp.inf))
                yh = _dot((cbm * dec).astype(BF16), xdt_b)
                yd = jnp.where(lane_half == hh, yh, yd)
            st = st_ref[pair]
            yoff = _dot(cg, st.astype(BF16)) * jnp.exp(cumx_p)
            dte = jnp.exp(cum_last - cumx_p)
            st_ref[pair] = st * jnp.exp(cum_last) + _dot_tn(bg, (xdt * dte).astype(BF16))
            y = yd + yoff + dx_ref[:, sl] * xs_p
            y_ref[:, sl] = y * _silu(z_ref[:, sl].astype(F32))
        gs = w // M2_GROUPS
        yg = y_ref[:, g * gs:(g + 1) * gs]
        ms = jnp.mean(yg * yg, axis=-1, keepdims=True)
        o_ref[:, g * gs:(g + 1) * gs] = (yg * lax.rsqrt(ms + EPS) * ng_ref[:, g * gs:(g + 1) * gs]).astype(o_ref.dtype)


def _m2_core(proj, dt_raw, conv_w, conv_b, dt_bias, a_log, d_skip, norm_g):
    s = proj.shape[0]
    L, w = M2_CHUNK, M2_WIDTH
    pad = LANES - M2_HEADS
    dtb = jnp.pad(dt_bias, (0, pad)).reshape(1, LANES)
    alog = jnp.pad(a_log, (0, pad)).reshape(1, LANES)
    dx = jnp.repeat(d_skip, M2_HEAD_DIM).reshape(1, w)
    ex = (jnp.arange(w)[None, :] // M2_HEAD_DIM == jnp.arange(LANES)[:, None]).astype(BF16)
    full = lambda shape: pl.BlockSpec(shape, lambda c: (0,) * len(shape))
    return pl.pallas_call(
        _m2_kernel,
        out_shape=jax.ShapeDtypeStruct((s, w), BF16),
        grid=(s // L,),
        in_specs=[pl.BlockSpec((L, w), lambda c: (c, 0)),
                  pl.BlockSpec((L, w), lambda c: (c, 1)),
                  pl.BlockSpec((L, w), lambda c: (c, 2)),
                  pl.BlockSpec((L, LANES), lambda c: (c, 0)),
                  full((M2_CONV, 2 * w)), full((1, 2 * w)), full((1, LANES)), full((1, LANES)),
                  full((1, w)), full((1, w)), full((LANES, w))],
        out_specs=pl.BlockSpec((L, w), lambda c: (c, 0)),
        scratch_shapes=[pltpu.VMEM((L + SUBLANES, 2 * w), F32),
                        pltpu.VMEM((L, 2 * w), F32),
                        pltpu.VMEM((M2_HEADS // 2, M2_STATE, LANES), F32),
                        pltpu.VMEM((LANES, L), F32),
                        pltpu.VMEM((L, w), F32), pltpu.VMEM((L, w), F32), pltpu.VMEM((L, w), F32)],
        compiler_params=_params("arbitrary"),
        name="mamba2_ssd",
    )(proj, proj, proj, dt_raw, conv_w, conv_b.reshape(1, -1), dtb, alog, dx, norm_g.reshape(1, -1), ex)


def _ret_kernel(q_ref, k_ref, v_ref, g_ref, ang_ref, lg_ref, o_ref, st_ref, cos_ref, sin_ref):
    c = pl.program_id(0)
    h = pl.program_id(1)
    L = RET_CHUNK
    half = RET_QK_DIM // 2

    @pl.when(c == 0)
    def _():
        st_ref[h] = jnp.zeros(st_ref.shape[1:], F32)

    @pl.when(h == 0)
    def _():
        pos = (c * L + lax.broadcasted_iota(jnp.int32, (L, half), 0)).astype(F32)
        ph = pos * ang_ref[...]
        cos_ref[...] = jnp.cos(ph)
        sin_ref[...] = jnp.sin(ph)

    cos, sin = cos_ref[...], sin_ref[...]

    def rot(x):
        x0, x1 = x[:, :half], x[:, half:]
        return jnp.concatenate([x0 * cos - x1 * sin, x1 * cos + x0 * sin], axis=1)

    q = rot(q_ref[...].astype(F32))
    k = rot(k_ref[...].astype(F32)) * (RET_QK_DIM ** -0.5)
    vb = v_ref[...].astype(BF16)
    lg = lg_ref[0]
    row = lax.broadcasted_iota(jnp.int32, (L, L), 0)
    col = lax.broadcasted_iota(jnp.int32, (L, L), 1)
    rel = (row - col).astype(F32)
    intra = jnp.where(rel >= 0, jnp.exp(lg * jnp.maximum(rel, 0.0)), 0.0)
    rowf = row.astype(F32)
    q_decay = jnp.exp(lg * (rowf + 1.0))
    k_decay = jnp.exp(lg * (L - 1.0 - rowf))
    qb = q.astype(BF16)
    scores = _dot_nt(qb, k.astype(BF16)) * intra
    st = st_ref[h]
    o_inter = _dot(qb, st.astype(BF16))
    o = _dot(scores.astype(BF16), vb) + jnp.concatenate([q_decay] * (RET_V_DIM // LANES), axis=1) * o_inter
    kd = k * jnp.concatenate([k_decay] * (RET_QK_DIM // LANES), axis=1)
    st_ref[h] = st * jnp.exp(lg[:, 0:1] * float(L)) + _dot_tn(kd.astype(BF16), vb)
    ms = jnp.mean(o * o, axis=-1, keepdims=True)
    o_ref[...] = (o * lax.rsqrt(ms + EPS) * _silu(g_ref[...].astype(F32))).astype(o_ref.dtype)


def _ret_core(proj):
    s = proj.shape[0]
    L = RET_CHUNK
    nh = RET_HEADS
    half = RET_QK_DIM // 2
    angle = (1.0 / (RET_ROT_BASE ** jnp.linspace(0.0, 1.0, half, dtype=F32))).reshape(1, half)
    log_g = jnp.log1p(-jnp.exp2(-5.0 - jnp.arange(nh, dtype=F32)))
    lg = jnp.broadcast_to(log_g[:, None, None], (nh, 1, LANES))
    vblk0 = 2 * D_MODEL // RET_V_DIM
    return pl.pallas_call(
        _ret_kernel,
        out_shape=jax.ShapeDtypeStruct((s, RET_V_WIDTH), BF16),
        grid=(s // L, nh),
        in_specs=[pl.BlockSpec((L, RET_QK_DIM), lambda c, h: (c, h)),
                  pl.BlockSpec((L, RET_QK_DIM), lambda c, h: (c, nh + h)),
                  pl.BlockSpec((L, RET_V_DIM), lambda c, h: (c, vblk0 + h)),
                  pl.BlockSpec((L, RET_V_DIM), lambda c, h: (c, vblk0 + nh + h)),
                  pl.BlockSpec((1, half), lambda c, h: (0, 0)),
                  pl.BlockSpec((1, 1, LANES), lambda c, h: (h, 0, 0))],
        out_specs=pl.BlockSpec((L, RET_V_DIM), lambda c, h: (c, h)),
        scratch_shapes=[pltpu.VMEM((nh, RET_QK_DIM, RET_V_DIM), F32),
                        pltpu.VMEM((L, half), F32), pltpu.VMEM((L, half), F32)],
        compiler_params=_params("arbitrary", "arbitrary"),
        name="retention",
    )(proj, proj, proj, proj, angle, lg)


def _ret_permute_qk(w):
    d = w.shape[0]
    qk = w[:, :2 * D_MODEL].reshape(d, 2 * RET_HEADS, RET_QK_DIM // 2, 2)
    qk = jnp.swapaxes(qk, 2, 3).reshape(d, 2 * D_MODEL)
    return jnp.concatenate([qk, w[:, 2 * D_MODEL:]], axis=1)


def kernel(x, norm_g, sb_w_in, sb_w_out, hg_w_in, hg_lb_logits, hg_norm_g, hg_w_out, m2_w_in, m2_conv_w,
           m2_conv_b, m2_dt_bias, m2_a_log, m2_d, m2_norm_g, m2_w_out, ret_w_in, ret_w_out, final_g):
    b, s, d = x.shape
    depth = norm_g.shape[0]
    lb_cum = jnp.cumsum(jax.nn.softmax(hg_lb_logits.astype(F32), axis=0), axis=0)
    lower_bounds = lb_cum - lb_cum[0]
    outs = []
    for bi in range(b):
        xb = x[bi]
        for i in range(depth):
            m, j = i % 4, i // 4
            fg = final_g if i == depth - 1 else None
            if m == 0:
                proj = _norm_proj(xb, norm_g[i], sb_w_in[j].astype(BF16), BF16)
                xb = _out_proj(_sb_core(proj), sb_w_out[j].astype(BF16), xb, fg)
            elif m == 1:
                proj = _norm_proj(xb, norm_g[i], hg_w_in[j].astype(BF16), BF16)
                og = _hgrn_core(proj, lower_bounds[i], hg_norm_g[j])
                xb = _out_proj(og, hg_w_out[j].astype(BF16), xb, fg)
            elif m == 2:
                n_main = M2_WIDTH + M2_WIDTH + 2 * M2_GROUPS * M2_STATE
                w_dt = jnp.pad(m2_w_in[j][:, n_main:], ((0, 0), (0, LANES - M2_HEADS))).astype(BF16)
                proj = _norm_proj(xb, norm_g[i], m2_w_in[j][:, :n_main].astype(BF16), BF16)
                dt_raw = _norm_proj(xb, norm_g[i], w_dt, F32, tn=LANES)
                og = _m2_core(proj, dt_raw, m2_conv_w[j], m2_conv_b[j], m2_dt_bias[j], m2_a_log[j], m2_d[j],
                              m2_norm_g[j])
                xb = _out_proj(og, m2_w_out[j].astype(BF16), xb, fg)
            else:
                w_in = _ret_permute_qk(ret_w_in[j]).astype(BF16)
                proj = _norm_proj(xb, norm_g[i], w_in, BF16)
                xb = _out_proj(_ret_core(proj), ret_w_out[j].astype(BF16), xb, fg)
        outs.append(xb)
    return jnp.stack(outs, axis=0)
```

```python
import functools
import math

import jax
import jax.numpy as jnp
from jax import lax
from jax.experimental import pallas as pl
from jax.experimental.pallas import tpu as pltpu

F32 = jnp.float32
BF16 = jnp.bfloat16
EPS = 1e-6
LANES = 128
SUBLANES = 8
VMEM_LIMIT = 48 * 1024 * 1024

D_MODEL = 1024
SB_HEADS, SB_HEAD_DIM, SB_BLOCK = 16, 64, 128
HG_HEADS, HG_HEAD_DIM, HG_CHUNK, HG_SUB = 8, 128, 64, 16
M2_WIDTH, M2_HEADS, M2_HEAD_DIM, M2_GROUPS, M2_STATE, M2_CONV, M2_CHUNK = 2048, 32, 64, 8, 128, 4, 128
RET_HEADS, RET_QK_DIM, RET_V_DIM, RET_V_WIDTH, RET_CHUNK = 4, 256, 512, 2048, 128
RET_ROT_BASE = 10000.0
SB_UNDERFLOW = -104.0
SB_WINDOW = 3
SB_TQ = 512


def _dot(a, b):
    return jnp.dot(a, b, preferred_element_type=F32)


def _dot_nt(a, b):
    return lax.dot_general(a, b, (((1,), (1,)), ((), ())), preferred_element_type=F32)


def _dot_tn(a, b):
    return lax.dot_general(a, b, (((0,), (0,)), ((), ())), preferred_element_type=F32)


def _split3(x):
    hi = x.astype(BF16)
    r1 = x - hi.astype(F32)
    mid = r1.astype(BF16)
    lo = (r1 - mid.astype(F32)).astype(BF16)
    return hi, mid, lo


def _dot_exact_right(x, m01):
    hi, mid, lo = _split3(x)
    return _dot(hi, m01) + _dot(mid, m01) + _dot(lo, m01)


def _dot_exact_left(m01, x):
    hi, mid, lo = _split3(x)
    return _dot(m01, hi) + _dot(m01, mid) + _dot(m01, lo)


def _sigmoid(x):
    return 1.0 / (1.0 + jnp.exp(-x))


def _silu(x):
    return x * _sigmoid(x)


def _softplus(x):
    return jnp.maximum(x, 0.0) + jnp.log1p(jnp.exp(-jnp.abs(x)))


def _params(*sem):
    return pltpu.CompilerParams(dimension_semantics=sem, vmem_limit_bytes=VMEM_LIMIT)


def _norm_proj_kernel(x_ref, g_ref, w_ref, o_ref, h_ref):
    @pl.when(pl.program_id(1) == 0)
    def _():
        xf = x_ref[...]
        ms = jnp.mean(xf * xf, axis=-1, keepdims=True)
        h_ref[...] = (xf * lax.rsqrt(ms + EPS) * g_ref[...]).astype(BF16)

    o_ref[...] = _dot(h_ref[...], w_ref[...]).astype(o_ref.dtype)


def _norm_proj(x, g, w, out_dtype, tn=1024, tm=1024):
    s, d = x.shape
    n = w.shape[1]
    tm = min(tm, s)
    return pl.pallas_call(
        _norm_proj_kernel,
        out_shape=jax.ShapeDtypeStruct((s, n), out_dtype),
        grid=(s // tm, n // tn),
        in_specs=[pl.BlockSpec((tm, d), lambda i, j: (i, 0)),
                  pl.BlockSpec((1, d), lambda i, j: (0, 0)),
                  pl.BlockSpec((d, tn), lambda i, j: (0, j))],
        out_specs=pl.BlockSpec((tm, tn), lambda i, j: (i, j)),
        scratch_shapes=[pltpu.VMEM((tm, d), BF16)],
        compiler_params=_params("arbitrary", "arbitrary"),
        name="norm_proj",
    )(x, g.reshape(1, d), w)


def _out_proj_kernel(og_ref, w_ref, x_ref, o_ref):
    o_ref[...] = x_ref[...] + _dot(og_ref[...], w_ref[...])


def _out_proj_final_kernel(og_ref, w_ref, x_ref, fg_ref, o_ref):
    y = x_ref[...] + _dot(og_ref[...], w_ref[...])
    ms = jnp.mean(y * y, axis=-1, keepdims=True)
    o_ref[...] = y * lax.rsqrt(ms + EPS) * fg_ref[...]


def _out_proj(og, w, x, final_g=None, tm=1024):
    s, width = og.shape
    tm = min(tm, s)
    d = x.shape[1]
    in_specs = [pl.BlockSpec((tm, width), lambda i: (i, 0)),
                pl.BlockSpec((width, d), lambda i: (0, 0)),
                pl.BlockSpec((tm, d), lambda i: (i, 0))]
    args = [og, w, x]
    body = _out_proj_kernel
    if final_g is not None:
        in_specs.append(pl.BlockSpec((1, d), lambda i: (0, 0)))
        args.append(final_g.reshape(1, d))
        body = _out_proj_final_kernel
    return pl.pallas_call(
        body,
        out_shape=jax.ShapeDtypeStruct((s, d), F32),
        grid=(s // tm,),
        in_specs=in_specs,
        out_specs=pl.BlockSpec((tm, d), lambda i: (i, 0)),
        compiler_params=_params("arbitrary"),
        name="out_proj",
    )(*args)


def _sb_spans(jobs, nblk, k_ref, v_ref, m_rev):
    rows = jobs[0][0].shape[0]
    width = nblk * SB_BLOCK
    blk = lambda x, b: x[:, b * SB_BLOCK:(b + 1) * SB_BLOCK]
    row = lax.broadcasted_iota(jnp.int32, (rows, SB_BLOCK), 0)
    lane = lax.broadcasted_iota(jnp.int32, (rows, SB_BLOCK), 1)
    rel = lane - (row & (SB_BLOCK - 1))
    starts = [pl.multiple_of(kblk0 * SB_BLOCK, SB_BLOCK) for _, kblk0, _, _ in jobs]
    zs = [_dot_nt(qs, k_ref[pl.ds(ks, width), :]) for (qs, _, _, _), ks in zip(jobs, starts)]
    masks, l1ms = [], []
    for (_, _, q0, _), ks, z in zip(jobs, starts, zs):
        mask = [rel < (q0 - ks - b * SB_BLOCK) for b in range(nblk)]
        l1m = [jnp.where(mask[b], -(jnp.maximum(blk(z, b), 0.0) + jnp.log(1.0 + jnp.exp(-jnp.abs(blk(z, b))))), 0.0)
               for b in range(nblk)]
        masks.append(mask)
        l1ms.append(l1m)
    revs = []
    for l1m in l1ms:
        revs.append(_dot_exact_right(jnp.concatenate(l1m, axis=0), m_rev))
    probs, cs = [], []
    for n, ((_, _, _, c), z) in enumerate(zip(jobs, zs)):
        parts = [None] * nblk
        for b in reversed(range(nblk)):
            e = jnp.exp(blk(z, b) + revs[n][b * rows:(b + 1) * rows] + c)
            parts[b] = jnp.where(masks[n][b], e, 0.0).astype(BF16)
            c = c + jnp.sum(l1ms[n][b], axis=1, keepdims=True)
        probs.append(jnp.concatenate(parts, axis=1))
        cs.append(c)
    return [(_dot(a, v_ref[pl.ds(ks, width), :]), c) for a, ks, c in zip(probs, starts, cs)]


def _sb_kernel(q_ref, k_ref, v_ref, g_ref, o_ref, acc_ref, c_ref):
    i = pl.program_id(1)
    njobs = q_ref.shape[0] // SB_BLOCK
    nkb = k_ref.shape[0] // SB_BLOCK
    n_half = LANES // SB_HEAD_DIM
    rows = n_half * SB_BLOCK
    row = lax.broadcasted_iota(jnp.int32, (SB_BLOCK, SB_BLOCK), 0)
    col = lax.broadcasted_iota(jnp.int32, (SB_BLOCK, SB_BLOCK), 1)
    m_rev = jnp.where(row >= col, 1.0, 0.0).astype(BF16)
    hmasks = [(col >= h * SB_HEAD_DIM) & (col < (h + 1) * SB_HEAD_DIM) for h in range(n_half)]
    jobs = []
    for n in range(njobs):
        q = q_ref[n * SB_BLOCK:(n + 1) * SB_BLOCK, :] * (SB_HEAD_DIM ** -0.5)
        qs = jnp.concatenate([jnp.where(m, q, jnp.zeros_like(q)) for m in hmasks], axis=0)
        qblk = i * njobs + n
        w0 = jnp.clip(qblk - (SB_WINDOW - 1), 0, nkb - SB_WINDOW)
        jobs.append((qs, w0, qblk * SB_BLOCK, jnp.zeros((rows, 1), F32)))
    results = _sb_spans(jobs, SB_WINDOW, k_ref, v_ref, m_rev)
    gate = g_ref[...].astype(F32)
    for n, (pv, c) in enumerate(results):
        acc_ref[n] = pv
        c_ref[n] = jnp.broadcast_to(c, c_ref.shape[1:])

    def live(d, cmaxes):
        return [jnp.logical_and(w0 - d >= 0, cm > SB_UNDERFLOW) for (_, w0, _, _), cm in zip(jobs, cmaxes)]

    def cond(st):
        return functools.reduce(jnp.logical_or, live(st[0], st[1:]))

    def body(st):
        d = st[0]
        step = []
        for n, (qs, w0, q0, _) in enumerate(jobs):
            inside = w0 - d >= 0
            step.append((qs, jnp.maximum(w0 - d, 0), jnp.where(inside, q0, -(1 << 30)), c_ref[n]))
        new = []
        for n, (pv, cn) in enumerate(_sb_spans(step, 1, k_ref, v_ref, m_rev)):
            acc_ref[n] += pv
            c_ref[n] = cn
            new.append(jnp.max(cn))
        return (d + 1, *new)

    lax.while_loop(cond, body, (jnp.int32(1), *[jnp.max(c) for _, c in results]))
    for n in range(njobs):
        out = acc_ref[n, 0:SB_BLOCK, :]
        for h in range(1, n_half):
            out = jnp.where(hmasks[h], acc_ref[n, h * SB_BLOCK:(h + 1) * SB_BLOCK, :], out)
        sl = slice(n * SB_BLOCK, (n + 1) * SB_BLOCK)
        o_ref[sl, :] = (out * _silu(gate[sl])).astype(o_ref.dtype)


def _sb_core(proj):
    s = proj.shape[0]
    nb = D_MODEL // LANES
    tq = SB_TQ
    return pl.pallas_call(
        _sb_kernel,
        out_shape=jax.ShapeDtypeStruct((s, D_MODEL), BF16),
        grid=(nb, s // tq),
        in_specs=[pl.BlockSpec((tq, LANES), lambda p, i: (i, p)),
                  pl.BlockSpec((s, LANES), lambda p, i: (0, nb + p)),
                  pl.BlockSpec((s, LANES), lambda p, i: (0, 2 * nb + p)),
                  pl.BlockSpec((tq, LANES), lambda p, i: (i, 3 * nb + p))],
        out_specs=pl.BlockSpec((tq, LANES), lambda p, i: (i, p)),
        scratch_shapes=[pltpu.VMEM((tq // SB_BLOCK, LANES // SB_HEAD_DIM * SB_BLOCK, LANES), F32)] * 2,
        compiler_params=_params("arbitrary", "arbitrary"),
        name="sb_attention",
    )(proj, proj, proj, proj)


def _hgrn_kernel(q_ref, f_ref, i_ref, g_ref, lb_ref, ng_ref, o_ref, st_ref, gc_ref, kc_ref, *, chunk):
    @pl.when(pl.program_id(1) == 0)
    def _():
        st_ref[...] = jnp.zeros_like(st_ref)

    c = chunk
    sub = HG_SUB
    nsub = c // sub
    row = lax.broadcasted_iota(jnp.int32, (c, c), 0)
    col = lax.broadcasted_iota(jnp.int32, (c, c), 1)
    tri = jnp.where(row >= col, 1.0, 0.0).astype(BF16)
    lane8 = lax.broadcasted_iota(jnp.int32, (SUBLANES, c), 1)
    krow = lax.broadcasted_iota(jnp.int32, (c, LANES), 0)
    lb = lb_ref[...]
    nchunk = q_ref.shape[0] // c
    chunks = [slice(ci * c, (ci + 1) * c) for ci in range(nchunk)]

    q = _silu(q_ref[...].astype(F32))
    f = lb + (1.0 - lb) * _sigmoid(f_ref[...].astype(F32))
    k = 1.0 - f
    vb = i_ref[...].astype(BF16)
    logf = jnp.log(f)
    g = jnp.concatenate([_dot_exact_left(tri, logf[ch]) for ch in chunks], axis=0)
    gc_ref[...] = g
    kc_ref[...] = k
    crosses = []
    for ch in chunks:
        per_sub = [jnp.zeros((sub, c), F32)]
        for si in range(1, nsub):
            lo = ch.start + si * sub
            ref = gc_ref[lo - 1:lo, :]
            kin = jnp.where(krow < si * sub, k[ch] * jnp.exp(jnp.minimum(ref - g[ch], 0.0)), 0.0)
            qin = q[lo:lo + sub, :] * jnp.exp(g[lo:lo + sub, :] - ref)
            per_sub.append(_dot_nt(qin.astype(BF16), kin.astype(BF16)))
        crosses.append(per_sub)
    o_intra = []
    for ch, per_sub in zip(chunks, crosses):
        blocks = []
        for si in range(nsub):
            for rb in range(sub // SUBLANES):
                r0 = ch.start + si * sub + rb * SUBLANES
                gb, qb = g[r0:r0 + SUBLANES, :], q[r0:r0 + SUBLANES, :]
                sb = per_sub[si][rb * SUBLANES:(rb + 1) * SUBLANES, :]
                for s in range(si * sub, si * sub + (rb + 1) * SUBLANES):
                    e = jnp.exp(jnp.minimum(gb - gc_ref[ch.start + s:ch.start + s + 1, :], 0.0))
                    colv = jnp.sum(qb * e * kc_ref[ch.start + s:ch.start + s + 1, :], axis=1, keepdims=True)
                    sb = jnp.where(lane8 == s, colv, sb)
                blocks.append(sb)
        sc = jnp.where(row >= col, jnp.concatenate(blocks, axis=0), 0.0)
        o_intra.append(_dot(sc.astype(BF16), vb[ch]))
    qd = (q * jnp.exp(g)).astype(BF16)
    updates, decays = [], []
    for ch in chunks:
        glast = gc_ref[ch.stop - 1:ch.stop, :]
        updates.append(_dot_tn(vb[ch], (k[ch] * jnp.exp(glast - g[ch])).astype(BF16)))
        decays.append(jnp.exp(glast))
    st = st_ref[...]
    outs = []
    for ci, ch in enumerate(chunks):
        outs.append(o_intra[ci] + _dot_nt(qd[ch], st.astype(BF16)))
        st = st * decays[ci] + updates[ci]
    st_ref[...] = st
    o = jnp.concatenate(outs, axis=0)
    ms = jnp.mean(o * o, axis=-1, keepdims=True)
    o_ref[...] = (o * lax.rsqrt(ms + EPS) * ng_ref[...] * _silu(g_ref[...].astype(F32))).astype(o_ref.dtype)


def _hgrn_core(proj, lb, norm_g, rows=512, chunk=HG_CHUNK):
    s = proj.shape[0]
    rows = min(rows, s)
    nh = HG_HEADS
    blk = lambda off: pl.BlockSpec((rows, LANES), lambda h, t: (t, off * nh + h))
    vec = pl.BlockSpec((1, LANES), lambda h, t: (0, h))
    return pl.pallas_call(
        functools.partial(_hgrn_kernel, chunk=chunk),
        out_shape=jax.ShapeDtypeStruct((s, D_MODEL), BF16),
        grid=(nh, s // rows),
        in_specs=[blk(0), blk(1), blk(2), blk(3), vec, vec],
        out_specs=pl.BlockSpec((rows, LANES), lambda h, t: (t, h)),
        scratch_shapes=[pltpu.VMEM((HG_HEAD_DIM, HG_HEAD_DIM), F32),
                        pltpu.VMEM((rows, LANES), F32), pltpu.VMEM((rows, LANES), F32)],
        compiler_params=_params("arbitrary", "arbitrary"),
        name="hgrn2",
    )(proj, proj, proj, proj, lb.reshape(1, -1), norm_g.reshape(1, -1))


def _m2_kernel(z_ref, xs_ref, bc_ref, dt_ref, cw_ref, cb_ref, dtb_ref, alog_ref, dx_ref, ng_ref, ex_ref,
               o_ref, xbuf_ref, act_ref, st_ref, cumt_ref, dtx_ref, cumx_ref, y_ref):
    L = M2_CHUNK
    w = M2_WIDTH
    slab = 512

    @pl.when(pl.program_id(0) == 0)
    def _():
        xbuf_ref[0:SUBLANES, :] = jnp.zeros((SUBLANES, 2 * w), F32)
        st_ref[...] = jnp.zeros_like(st_ref)

    xbuf_ref[SUBLANES:SUBLANES + L, 0:w] = xs_ref[...].astype(F32)
    xbuf_ref[SUBLANES:SUBLANES + L, w:2 * w] = bc_ref[...].astype(F32)
    for c0 in range(0, 2 * w, slab):
        xw = xbuf_ref[:, c0:c0 + slab]
        acc = cb_ref[:, c0:c0 + slab] + cw_ref[M2_CONV - 1:M2_CONV, c0:c0 + slab] * xw[SUBLANES:, :]
        for back in range(1, M2_CONV):
            shifted = pltpu.roll(xw, back, axis=0)[SUBLANES:, :]
            acc = acc + cw_ref[M2_CONV - 1 - back:M2_CONV - back, c0:c0 + slab] * shifted
        act_ref[:, c0:c0 + slab] = _silu(acc)
    xbuf_ref[0:SUBLANES, :] = xbuf_ref[L:L + SUBLANES, :]

    row = lax.broadcasted_iota(jnp.int32, (L, L), 0)
    col = lax.broadcasted_iota(jnp.int32, (L, L), 1)
    tril = row >= col
    tri = jnp.where(tril, 1.0, 0.0).astype(BF16)
    dt = _softplus(dt_ref[...] + dtb_ref[...])
    adt = dt * (-jnp.exp(alog_ref[...]))
    cum = _dot_exact_left(tri, adt)
    cumt_ref[...] = cum.T
    ex = ex_ref[...]
    dtx_ref[...] = _dot_exact_right(dt, ex)
    cumx_ref[...] = _dot_exact_right(cum, ex)
    lane_half = lax.broadcasted_iota(jnp.int32, (L, LANES), 1) // M2_HEAD_DIM

    for g in range(M2_GROUPS):
        bg = act_ref[:, w + g * M2_STATE: w + (g + 1) * M2_STATE].astype(BF16)
        cg = act_ref[:, w + (M2_GROUPS + g) * M2_STATE: w + (M2_GROUPS + g + 1) * M2_STATE].astype(BF16)
        cbm = _dot_nt(cg, bg)
        heads_per_group = M2_HEADS // M2_GROUPS
        for pp in range(heads_per_group // 2):
            pair = g * (heads_per_group // 2) + pp
            sl = slice(pair * LANES, (pair + 1) * LANES)
            xs_p = act_ref[:, sl]
            xdt = xs_p * dtx_ref[:, sl]
            xdt_b = xdt.astype(BF16)
            cumx_p = cumx_ref[:, sl]
            cum_last = cumx_ref[L - 1:L, sl]
            yd = jnp.zeros((L, LANES), F32)
            for hh in range(2):
                h = 2 * pair + hh
                cc = cumx_ref[:, h * M2_HEAD_DIM:h * M2_HEAD_DIM + 1]
                cr = cumt_ref[h:h + 1, :]
                dec = jnp.exp(jnp.where(tril, cc - cr, -jnp.inf))
                yh = _dot((cbm * dec).astype(BF16), xdt_b)
                yd = jnp.where(lane_half == hh, yh, yd)
            st = st_ref[pair]
            yoff = _dot(cg, st.astype(BF16)) * jnp.exp(cumx_p)
            dte = jnp.exp(cum_last - cumx_p)
            st_ref[pair] = st * jnp.exp(cum_last) + _dot_tn(bg, (xdt * dte).astype(BF16))
            y = yd + yoff + dx_ref[:, sl] * xs_p
            y_ref[:, sl] = y * _silu(z_ref[:, sl].astype(F32))
        gs = w // M2_GROUPS
        yg = y_ref[:, g * gs:(g + 1) * gs]
        ms = jnp.mean(yg * yg, axis=-1, keepdims=True)
        o_ref[:, g * gs:(g + 1) * gs] = (yg * lax.rsqrt(ms + EPS) * ng_ref[:, g * gs:(g + 1) * gs]).astype(o_ref.dtype)


def _m2_core(proj, dt_raw, conv_w, conv_b, dt_bias, a_log, d_skip, norm_g):
    s = proj.shape[0]
    L, w = M2_CHUNK, M2_WIDTH
    pad = LANES - M2_HEADS
    dtb = jnp.pad(dt_bias, (0, pad)).reshape(1, LANES)
    alog = jnp.pad(a_log, (0, pad)).reshape(1, LANES)
    dx = jnp.repeat(d_skip, M2_HEAD_DIM).reshape(1, w)
    ex = (jnp.arange(w)[None, :] // M2_HEAD_DIM == jnp.arange(LANES)[:, None]).astype(BF16)
    full = lambda shape: pl.BlockSpec(shape, lambda c: (0,) * len(shape))
    return pl.pallas_call(
        _m2_kernel,
        out_shape=jax.ShapeDtypeStruct((s, w), BF16),
        grid=(s // L,),
        in_specs=[pl.BlockSpec((L, w), lambda c: (c, 0)),
                  pl.BlockSpec((L, w), lambda c: (c, 1)),
                  pl.BlockSpec((L, w), lambda c: (c, 2)),
                  pl.BlockSpec((L, LANES), lambda c: (c, 0)),
                  full((M2_CONV, 2 * w)), full((1, 2 * w)), full((1, LANES)), full((1, LANES)),
                  full((1, w)), full((1, w)), full((LANES, w))],
        out_specs=pl.BlockSpec((L, w), lambda c: (c, 0)),
        scratch_shapes=[pltpu.VMEM((L + SUBLANES, 2 * w), F32),
                        pltpu.VMEM((L, 2 * w), F32),
                        pltpu.VMEM((M2_HEADS // 2, M2_STATE, LANES), F32),
                        pltpu.VMEM((LANES, L), F32),
                        pltpu.VMEM((L, w), F32), pltpu.VMEM((L, w), F32), pltpu.VMEM((L, w), F32)],
        compiler_params=_params("arbitrary"),
        name="mamba2_ssd",
    )(proj, proj, proj, dt_raw, conv_w, conv_b.reshape(1, -1), dtb, alog, dx, norm_g.reshape(1, -1), ex)


def _ret_kernel(q_ref, k_ref, v_ref, g_ref, ang_ref, lg_ref, o_ref, st_ref, intra_ref, qdec_ref, kdec_ref):
    c = pl.program_id(0)
    L = RET_CHUNK
    half = RET_QK_DIM // 2
    heads = range(RET_HEADS)

    @pl.when(c == 0)
    def _():
        st_ref[...] = jnp.zeros_like(st_ref)
        row = lax.broadcasted_iota(jnp.int32, (L, L), 0)
        col = lax.broadcasted_iota(jnp.int32, (L, L), 1)
        rel = (row - col).astype(F32)
        rowf = row.astype(F32)
        for h in heads:
            lg = lg_ref[h]
            intra_ref[h] = jnp.where(rel >= 0, jnp.exp(lg * jnp.maximum(rel, 0.0)), 0.0)
            qdec_ref[h] = jnp.exp(lg * (rowf + 1.0))
            kdec_ref[h] = jnp.exp(lg * (L - 1.0 - rowf))

    pos = (c * L + lax.broadcasted_iota(jnp.int32, (L, half), 0)).astype(F32)
    ph = pos * ang_ref[...]
    cos, sin = jnp.cos(ph), jnp.sin(ph)

    def rot(x):
        x0, x1 = x[:, :half], x[:, half:]
        return jnp.concatenate([x0 * cos - x1 * sin, x1 * cos + x0 * sin], axis=1)

    qk = lambda ref, h: rot(ref[:, h * RET_QK_DIM:(h + 1) * RET_QK_DIM].astype(F32))
    vcols = [slice(h * RET_V_DIM, (h + 1) * RET_V_DIM) for h in heads]
    qbs = [qk(q_ref, h).astype(BF16) for h in heads]
    ks = [qk(k_ref, h) * (RET_QK_DIM ** -0.5) for h in heads]
    vbs = [v_ref[:, vcols[h]].astype(BF16) for h in heads]
    scores = [(_dot_nt(qbs[h], ks[h].astype(BF16)) * intra_ref[h]).astype(BF16) for h in heads]
    inters = [_dot(qbs[h], st_ref[h].astype(BF16)) for h in heads]
    for h in heads:
        kd = ks[h] * jnp.concatenate([kdec_ref[h]] * (RET_QK_DIM // LANES), axis=1)
        st_ref[h] = st_ref[h] * jnp.exp(lg_ref[h][:, 0:1] * float(L)) + _dot_tn(kd.astype(BF16), vbs[h])
    for h in heads:
        o = _dot(scores[h], vbs[h]) + jnp.concatenate([qdec_ref[h]] * (RET_V_DIM // LANES), axis=1) * inters[h]
        ms = jnp.mean(o * o, axis=-1, keepdims=True)
        o_ref[:, vcols[h]] = (o * lax.rsqrt(ms + EPS) * _silu(g_ref[:, vcols[h]].astype(F32))).astype(o_ref.dtype)


def _ret_core(proj):
    s = proj.shape[0]
    L = RET_CHUNK
    nh = RET_HEADS
    half = RET_QK_DIM // 2
    angle = (1.0 / (RET_ROT_BASE ** jnp.linspace(0.0, 1.0, half, dtype=F32))).reshape(1, half)
    log_g = jnp.log1p(-jnp.exp2(-5.0 - jnp.arange(nh, dtype=F32)))
    lg = jnp.broadcast_to(log_g[:, None, None], (nh, 1, LANES))
    return pl.pallas_call(
        _ret_kernel,
        out_shape=jax.ShapeDtypeStruct((s, RET_V_WIDTH), BF16),
        grid=(s // L,),
        in_specs=[pl.BlockSpec((L, D_MODEL), lambda c: (c, 0)),
                  pl.BlockSpec((L, D_MODEL), lambda c: (c, 1)),
                  pl.BlockSpec((L, RET_V_WIDTH), lambda c: (c, 1)),
                  pl.BlockSpec((L, RET_V_WIDTH), lambda c: (c, 2)),
                  pl.BlockSpec((1, half), lambda c: (0, 0)),
                  pl.BlockSpec((nh, 1, LANES), lambda c: (0, 0, 0))],
        out_specs=pl.BlockSpec((L, RET_V_WIDTH), lambda c: (c, 0)),
        scratch_shapes=[pltpu.VMEM((nh, RET_QK_DIM, RET_V_DIM), F32),
                        pltpu.VMEM((nh, L, L), F32), pltpu.VMEM((nh, L, LANES), F32),
                        pltpu.VMEM((nh, L, LANES), F32)],
        compiler_params=_params("arbitrary"),
        name="retention",
    )(proj, proj, proj, proj, angle, lg)


def _ret_permute_qk(w):
    d = w.shape[0]
    qk = w[:, :2 * D_MODEL].reshape(d, 2 * RET_HEADS, RET_QK_DIM // 2, 2)
    qk = jnp.swapaxes(qk, 2, 3).reshape(d, 2 * D_MODEL)
    return jnp.concatenate([qk, w[:, 2 * D_MODEL:]], axis=1)


def kernel(x, norm_g, sb_w_in, sb_w_out, hg_w_in, hg_lb_logits, hg_norm_g, hg_w_out, m2_w_in, m2_conv_w,
           m2_conv_b, m2_dt_bias, m2_a_log, m2_d, m2_norm_g, m2_w_out, ret_w_in, ret_w_out, final_g):
    b, s, d = x.shape
    depth = norm_g.shape[0]
    lb_cum = jnp.cumsum(jax.nn.softmax(hg_lb_logits.astype(F32), axis=0), axis=0)
    lower_bounds = lb_cum - lb_cum[0]
    outs = []
    for bi in range(b):
        xb = x[bi]
        for i in range(depth):
            m, j = i % 4, i // 4
            fg = final_g if i == depth - 1 else None
            if m == 0:
                proj = _norm_proj(xb, norm_g[i], sb_w_in[j].astype(BF16), BF16)
                xb = _out_proj(_sb_core(proj), sb_w_out[j].astype(BF16), xb, fg)
            elif m == 1:
                proj = _norm_proj(xb, norm_g[i], hg_w_in[j].astype(BF16), BF16)
                og = _hgrn_core(proj, lower_bounds[i], hg_norm_g[j])
                xb = _out_proj(og, hg_w_out[j].astype(BF16), xb, fg)
            elif m == 2:
                n_main = M2_WIDTH + M2_WIDTH + 2 * M2_GROUPS * M2_STATE
                w_dt = jnp.pad(m2_w_in[j][:, n_main:], ((0, 0), (0, LANES - M2_HEADS))).astype(BF16)
                proj = _norm_proj(xb, norm_g[i], m2_w_in[j][:, :n_main].astype(BF16), BF16)
                dt_raw = _norm_proj(xb, norm_g[i], w_dt, F32, tn=LANES)
                og = _m2_core(proj, dt_raw, m2_conv_w[j], m2_conv_b[j], m2_dt_bias[j], m2_a_log[j], m2_d[j],
                              m2_norm_g[j])
                xb = _out_proj(og, m2_w_out[j].astype(BF16), xb, fg)
            else:
                w_in = _ret_permute_qk(ret_w_in[j]).astype(BF16)
                proj = _norm_proj(xb, norm_g[i], w_in, BF16)
                xb = _out_proj(_ret_core(proj), ret_w_out[j].astype(BF16), xb, fg)
        outs.append(xb)
    return jnp.stack(outs, axis=0)
```

```python
import functools
import math

import jax
import jax.numpy as jnp
from jax import lax
from jax.experimental import pallas as pl
from jax.experimental.pallas import tpu as pltpu

F32 = jnp.float32
BF16 = jnp.bfloat16
EPS = 1e-6
LANES = 128
SUBLANES = 8
MXU_COLS = 256
VMEM_LIMIT = 48 * 1024 * 1024

D_MODEL = 1024
SB_HEADS, SB_HEAD_DIM, SB_BLOCK = 16, 64, 128
HG_HEADS, HG_HEAD_DIM, HG_CHUNK, HG_SUB = 8, 128, 64, 16
M2_WIDTH, M2_HEADS, M2_HEAD_DIM, M2_GROUPS, M2_STATE, M2_CONV, M2_CHUNK = 2048, 32, 64, 8, 128, 4, 128
RET_HEADS, RET_QK_DIM, RET_V_DIM, RET_V_WIDTH, RET_CHUNK = 4, 256, 512, 2048, 128
RET_ROT_BASE = 10000.0
SB_UNDERFLOW = -104.0
SB_QROWS = 64
SB_WINDOW = 2
SB_TQ = 512


def _dot(a, b):
    return jnp.dot(a, b, preferred_element_type=F32)


def _dot_nt(a, b):
    return lax.dot_general(a, b, (((1,), (1,)), ((), ())), preferred_element_type=F32)


def _dot_tn(a, b):
    return lax.dot_general(a, b, (((0,), (0,)), ((), ())), preferred_element_type=F32)


def _split3(x):
    hi = x.astype(BF16)
    r1 = x - hi.astype(F32)
    mid = r1.astype(BF16)
    lo = (r1 - mid.astype(F32)).astype(BF16)
    return hi, mid, lo


def _dot_exact_right(x, m01):
    hi, mid, lo = _split3(x)
    return _dot(hi, m01) + _dot(mid, m01) + _dot(lo, m01)


def _dot_exact_left(m01, x):
    hi, mid, lo = _split3(x)
    return _dot(m01, hi) + _dot(m01, mid) + _dot(m01, lo)


def _sigmoid(x):
    return 1.0 / (1.0 + jnp.exp(-x))


def _silu(x):
    return x * _sigmoid(x)


def _softplus(x):
    u = jnp.exp(-jnp.abs(x))
    w = 1.0 + u
    log1p_u = jnp.where(w == 1.0, u, jnp.log(w) * (u / (w - 1.0)))
    return jnp.maximum(x, 0.0) + log1p_u


def _params(*sem):
    return pltpu.CompilerParams(dimension_semantics=sem, vmem_limit_bytes=VMEM_LIMIT)


def _rms_to_bf16(x_ref, g_ref, h_ref):
    xf = x_ref[...]
    ms = jnp.mean(xf * xf, axis=-1, keepdims=True)
    h_ref[...] = (xf * lax.rsqrt(ms + EPS) * g_ref[...]).astype(BF16)


def _norm_proj_kernel(x_ref, g_ref, w_ref, o_ref, h_ref):
    @pl.when(pl.program_id(1) == 0)
    def _():
        _rms_to_bf16(x_ref, g_ref, h_ref)

    o_ref[...] = _dot(h_ref[...], w_ref[...]).astype(o_ref.dtype)


def _norm_proj(x, g, w, tn=2048, tm=1024):
    s, d = x.shape
    n = w.shape[1]
    tm = min(tm, s)
    return pl.pallas_call(
        _norm_proj_kernel,
        out_shape=jax.ShapeDtypeStruct((s, n), BF16),
        grid=(s // tm, n // tn),
        in_specs=[pl.BlockSpec((tm, d), lambda i, j: (i, 0)),
                  pl.BlockSpec((1, d), lambda i, j: (0, 0)),
                  pl.BlockSpec((d, tn), lambda i, j: (0, j))],
        out_specs=pl.BlockSpec((tm, tn), lambda i, j: (i, j)),
        scratch_shapes=[pltpu.VMEM((tm, d), BF16)],
        compiler_params=_params("arbitrary", "arbitrary"),
        name="norm_proj",
    )(x, g.reshape(1, d), w)


def _m2_proj_kernel(x_ref, g_ref, w_ref, dtw_ref, o_ref, dt_ref, h_ref):
    @pl.when(pl.program_id(1) == 0)
    def _():
        _rms_to_bf16(x_ref, g_ref, h_ref)
        dt_ref[...] = _dot(h_ref[...], dtw_ref[...])

    o_ref[...] = _dot(h_ref[...], w_ref[...]).astype(o_ref.dtype)


def _m2_in_proj(x, g, w, w_dt, n, tn=2048, tm=1024):
    s, d = x.shape
    tm = min(tm, s)
    return pl.pallas_call(
        _m2_proj_kernel,
        out_shape=(jax.ShapeDtypeStruct((s, n), BF16), jax.ShapeDtypeStruct((s, LANES), F32)),
        grid=(s // tm, n // tn),
        in_specs=[pl.BlockSpec((tm, d), lambda i, j: (i, 0)),
                  pl.BlockSpec((1, d), lambda i, j: (0, 0)),
                  pl.BlockSpec((d, tn), lambda i, j: (0, j)),
                  pl.BlockSpec((d, LANES), lambda i, j: (0, 0))],
        out_specs=(pl.BlockSpec((tm, tn), lambda i, j: (i, j)),
                   pl.BlockSpec((tm, LANES), lambda i, j: (i, 0))),
        scratch_shapes=[pltpu.VMEM((tm, d), BF16)],
        compiler_params=_params("arbitrary", "arbitrary"),
        name="m2_in_proj",
    )(x, g.reshape(1, d), w, w_dt)


def _out_proj_kernel(og_ref, w_ref, x_ref, o_ref):
    o_ref[...] = x_ref[...] + _dot(og_ref[...], w_ref[...])


def _out_proj_final_kernel(og_ref, w_ref, x_ref, fg_ref, o_ref):
    y = x_ref[...] + _dot(og_ref[...], w_ref[...])
    ms = jnp.mean(y * y, axis=-1, keepdims=True)
    o_ref[...] = y * lax.rsqrt(ms + EPS) * fg_ref[...]


def _out_proj(og, w, x, final_g=None, tm=1024):
    s, width = og.shape
    tm = min(tm, s)
    d = x.shape[1]
    in_specs = [pl.BlockSpec((tm, width), lambda i: (i, 0)),
                pl.BlockSpec((width, d), lambda i: (0, 0)),
                pl.BlockSpec((tm, d), lambda i: (i, 0))]
    args = [og, w, x]
    body = _out_proj_kernel
    if final_g is not None:
        in_specs.append(pl.BlockSpec((1, d), lambda i: (0, 0)))
        args.append(final_g.reshape(1, d))
        body = _out_proj_final_kernel
    return pl.pallas_call(
        body,
        out_shape=jax.ShapeDtypeStruct((s, d), F32),
        grid=(s // tm,),
        in_specs=in_specs,
        out_specs=pl.BlockSpec((tm, d), lambda i: (i, 0)),
        compiler_params=_params("arbitrary"),
        name="out_proj",
    )(*args)


def _sb_spans(jobs, nblk, rel, k_ref, v_ref, m_rev):
    rows = jobs[0][0].shape[0]
    width = nblk * SB_BLOCK
    blk = lambda x, b: x[:, b * SB_BLOCK:(b + 1) * SB_BLOCK]
    starts = [pl.multiple_of(ks, SB_QROWS) for _, ks, _, _ in jobs]
    zs = [_dot_nt(qs, k_ref[pl.ds(ks, width), :]) for (qs, _, _, _), ks in zip(jobs, starts)]
    masks, l1ms = [], []
    for (_, _, bound, _), z in zip(jobs, zs):
        mask = [rel < (bound - b * SB_BLOCK) for b in range(nblk)]
        l1m = [jnp.where(mask[b], -(jnp.maximum(blk(z, b), 0.0) + jnp.log(1.0 + jnp.exp(-jnp.abs(blk(z, b))))), 0.0)
               for b in range(nblk)]
        masks.append(mask)
        l1ms.append(l1m)
    revs = []
    for l1m in l1ms:
        stacked = jnp.concatenate(l1m, axis=0)
        hi = stacked.astype(BF16)
        lo = (stacked - hi.astype(F32)).astype(BF16)
        revs.append(_dot(hi, m_rev) + _dot(lo, m_rev))
    probs, cs = [], []
    for n, ((_, _, _, c), z) in enumerate(zip(jobs, zs)):
        parts = [None] * nblk
        for b in reversed(range(nblk)):
            e = jnp.exp(blk(z, b) + revs[n][b * rows:(b + 1) * rows] + c)
            parts[b] = jnp.where(masks[n][b], e, 0.0).astype(BF16)
            c = c + jnp.sum(l1ms[n][b], axis=1, keepdims=True)
        probs.append(jnp.concatenate(parts, axis=1))
        cs.append(c)
    return [(_dot(a, v_ref[pl.ds(ks, width), :]), c) for a, ks, c in zip(probs, starts, cs)]


def _sb_kernel(q_ref, k_ref, v_ref, g_ref, o_ref, acc_ref, c_ref):
    i = pl.program_id(1)
    njobs = q_ref.shape[0] // SB_QROWS
    nkeys = k_ref.shape[0]
    n_half = LANES // SB_HEAD_DIM
    rows = n_half * SB_QROWS
    span = SB_WINDOW * SB_BLOCK
    row = lax.broadcasted_iota(jnp.int32, (SB_BLOCK, SB_BLOCK), 0)
    col = lax.broadcasted_iota(jnp.int32, (SB_BLOCK, SB_BLOCK), 1)
    m_rev = jnp.where(row >= col, 1.0, 0.0).astype(BF16)
    lane_q = lax.broadcasted_iota(jnp.int32, (SB_QROWS, LANES), 1)
    hmasks = [(lane_q >= h * SB_HEAD_DIM) & (lane_q < (h + 1) * SB_HEAD_DIM) for h in range(n_half)]
    lane = lax.broadcasted_iota(jnp.int32, (rows, SB_BLOCK), 1)
    qoff = lax.broadcasted_iota(jnp.int32, (rows, SB_BLOCK), 0) & (SB_QROWS - 1)
    jobs = []
    for n in range(njobs):
        q = q_ref[n * SB_QROWS:(n + 1) * SB_QROWS, :] * (SB_HEAD_DIM ** -0.5)
        qs = jnp.concatenate([jnp.where(m, q, jnp.zeros_like(q)) for m in hmasks], axis=0)
        q0 = (i * njobs + n) * SB_QROWS
        ks = jnp.clip(q0 + SB_QROWS - span, 0, nkeys - span)
        jobs.append((qs, ks, q0 - ks, jnp.zeros((rows, 1), F32)))
    results = _sb_spans(jobs, SB_WINDOW, lane - qoff, k_ref, v_ref, m_rev)
    gate = g_ref[...].astype(F32)
    for n, (pv, c) in enumerate(results):
        acc_ref[n] = pv
        c_ref[n] = jnp.broadcast_to(c, c_ref.shape[1:])

    def cond(st):
        ends, cmaxes = st[:njobs], st[njobs:]
        return functools.reduce(jnp.logical_or, [jnp.logical_and(e > 0, cm > SB_UNDERFLOW)
                                                 for e, cm in zip(ends, cmaxes)])

    def body(st):
        ends = st[:njobs]
        starts = [jnp.maximum(e - SB_BLOCK, 0) for e in ends]
        step = [(qs, s0, e - s0, c_ref[n]) for n, ((qs, _, _, _), s0, e) in enumerate(zip(jobs, starts, ends))]
        new = []
        for n, (pv, cn) in enumerate(_sb_spans(step, 1, lane, k_ref, v_ref, m_rev)):
            acc_ref[n] += pv
            c_ref[n] = cn
            new.append(jnp.max(cn))
        return (*starts, *new)

    lax.while_loop(cond, body, (*[ks for _, ks, _, _ in jobs], *[jnp.max(c) for _, c in results]))
    for n in range(njobs):
        out = acc_ref[n, 0:SB_QROWS, :]
        for h in range(1, n_half):
            out = jnp.where(hmasks[h], acc_ref[n, h * SB_QROWS:(h + 1) * SB_QROWS, :], out)
        sl = slice(n * SB_QROWS, (n + 1) * SB_QROWS)
        o_ref[sl, :] = (out * _silu(gate[sl])).astype(o_ref.dtype)


def _sb_core(proj):
    s = proj.shape[0]
    nb = D_MODEL // LANES
    tq = SB_TQ
    return pl.pallas_call(
        _sb_kernel,
        out_shape=jax.ShapeDtypeStruct((s, D_MODEL), BF16),
        grid=(nb, s // tq),
        in_specs=[pl.BlockSpec((tq, LANES), lambda p, i: (i, p)),
                  pl.BlockSpec((s, LANES), lambda p, i: (0, nb + p)),
                  pl.BlockSpec((s, LANES), lambda p, i: (0, 2 * nb + p)),
                  pl.BlockSpec((tq, LANES), lambda p, i: (i, 3 * nb + p))],
        out_specs=pl.BlockSpec((tq, LANES), lambda p, i: (i, p)),
        scratch_shapes=[pltpu.VMEM((tq // SB_QROWS, LANES // SB_HEAD_DIM * SB_QROWS, LANES), F32)] * 2,
        compiler_params=_params("arbitrary", "arbitrary"),
        name="sb_attention",
    )(proj, proj, proj, proj)


def _hgrn_kernel(q_ref, f_ref, i_ref, g_ref, lb_ref, ng_ref, o_ref, st_ref, gc_ref, kc_ref, *, chunk):
    @pl.when(pl.program_id(1) == 0)
    def _():
        st_ref[...] = jnp.zeros_like(st_ref)

    c = chunk
    sub = HG_SUB
    nsub = c // sub
    row = lax.broadcasted_iota(jnp.int32, (c, c), 0)
    col = lax.broadcasted_iota(jnp.int32, (c, c), 1)
    tri = jnp.where(row >= col, 1.0, 0.0).astype(BF16)
    lane8 = lax.broadcasted_iota(jnp.int32, (SUBLANES, c), 1)
    krow = lax.broadcasted_iota(jnp.int32, (c, LANES), 0)
    lb = lb_ref[...]
    nchunk = q_ref.shape[0] // c
    chunks = [slice(ci * c, (ci + 1) * c) for ci in range(nchunk)]

    q = _silu(q_ref[...].astype(F32))
    f = lb + (1.0 - lb) * _sigmoid(f_ref[...].astype(F32))
    k = 1.0 - f
    vb = i_ref[...].astype(BF16)
    logf = jnp.log(f)
    g = jnp.concatenate([_dot_exact_left(tri, logf[ch]) for ch in chunks], axis=0)
    gc_ref[...] = g
    kc_ref[...] = k
    crosses = []
    for ch in chunks:
        per_sub = [jnp.zeros((sub, c), F32)]
        for si in range(1, nsub):
            lo = ch.start + si * sub
            ref = gc_ref[lo - 1:lo, :]
            kin = jnp.where(krow < si * sub, k[ch] * jnp.exp(ref - g[ch]), 0.0)
            qin = q[lo:lo + sub, :] * jnp.exp(g[lo:lo + sub, :] - ref)
            per_sub.append(_dot_nt(qin.astype(BF16), kin.astype(BF16)))
        crosses.append(per_sub)
    o_intra = []
    for ch, per_sub in zip(chunks, crosses):
        blocks = []
        for si in range(nsub):
            for rb in range(sub // SUBLANES):
                r0 = ch.start + si * sub + rb * SUBLANES
                gb, qb = g[r0:r0 + SUBLANES, :], q[r0:r0 + SUBLANES, :]
                sb = per_sub[si][rb * SUBLANES:(rb + 1) * SUBLANES, :]
                for s in range(si * sub, si * sub + (rb + 1) * SUBLANES):
                    e = jnp.exp(gb - gc_ref[ch.start + s:ch.start + s + 1, :])
                    colv = jnp.sum(qb * e * kc_ref[ch.start + s:ch.start + s + 1, :], axis=1, keepdims=True)
                    sb = jnp.where(lane8 == s, colv, sb)
                blocks.append(sb)
        sc = jnp.where(row >= col, jnp.concatenate(blocks, axis=0), 0.0)
        o_intra.append(_dot(sc.astype(BF16), vb[ch]))
    qd = (q * jnp.exp(g)).astype(BF16)
    updates, decays = [], []
    for ch in chunks:
        glast = gc_ref[ch.stop - 1:ch.stop, :]
        updates.append(_dot_tn(vb[ch], (k[ch] * jnp.exp(glast - g[ch])).astype(BF16)))
        decays.append(jnp.exp(glast))
    st = st_ref[...]
    outs = []
    for ci, ch in enumerate(chunks):
        outs.append(o_intra[ci] + _dot_nt(qd[ch], st.astype(BF16)))
        st = st * decays[ci] + updates[ci]
    st_ref[...] = st
    o = jnp.concatenate(outs, axis=0)
    ms = jnp.mean(o * o, axis=-1, keepdims=True)
    o_ref[...] = (o * lax.rsqrt(ms + EPS) * ng_ref[...] * _silu(g_ref[...].astype(F32))).astype(o_ref.dtype)


def _hgrn_core(proj, lb, norm_g, rows=512, chunk=HG_CHUNK):
    s = proj.shape[0]
    rows = min(rows, s)
    nh = HG_HEADS
    blk = lambda off: pl.BlockSpec((rows, LANES), lambda h, t: (t, off * nh + h))
    vec = pl.BlockSpec((1, LANES), lambda h, t: (0, h))
    return pl.pallas_call(
        functools.partial(_hgrn_kernel, chunk=chunk),
        out_shape=jax.ShapeDtypeStruct((s, D_MODEL), BF16),
        grid=(nh, s // rows),
        in_specs=[blk(0), blk(1), blk(2), blk(3), vec, vec],
        out_specs=pl.BlockSpec((rows, LANES), lambda h, t: (t, h)),
        scratch_shapes=[pltpu.VMEM((HG_HEAD_DIM, HG_HEAD_DIM), F32),
                        pltpu.VMEM((rows, LANES), F32), pltpu.VMEM((rows, LANES), F32)],
        compiler_params=_params("arbitrary", "arbitrary"),
        name="hgrn2",
    )(proj, proj, proj, proj, lb.reshape(1, -1), norm_g.reshape(1, -1))


def _m2_kernel(z_ref, xs_ref, bc_ref, dt_ref, cw_ref, cb_ref, dtb_ref, alog_ref, dx_ref, ng_ref, ex_ref,
               o_ref, xbuf_ref, act_ref, st_ref, cumt_ref, dtx_ref, cumx_ref, y_ref):
    L = M2_CHUNK
    w = M2_WIDTH
    slab = 512

    @pl.when(pl.program_id(0) == 0)
    def _():
        xbuf_ref[0:SUBLANES, :] = jnp.zeros((SUBLANES, 2 * w), F32)
        st_ref[...] = jnp.zeros_like(st_ref)

    xbuf_ref[SUBLANES:SUBLANES + L, 0:w] = xs_ref[...].astype(F32)
    xbuf_ref[SUBLANES:SUBLANES + L, w:2 * w] = bc_ref[...].astype(F32)
    sub = lax.broadcasted_iota(jnp.int32, (L // SUBLANES, SUBLANES, slab), 1)
    for c0 in range(0, 2 * w, slab):
        cols = slice(c0, c0 + slab)
        window = xbuf_ref[:, cols].reshape(L // SUBLANES + 1, SUBLANES, slab)
        acc = cb_ref[:, cols] + cw_ref[M2_CONV - 1:M2_CONV, cols] * xbuf_ref[SUBLANES:, cols]
        for back in range(1, M2_CONV):
            rot = pltpu.roll(window, back, axis=1)
            shifted = jnp.where(sub < back, rot[:-1], rot[1:]).reshape(L, slab)
            acc = acc + cw_ref[M2_CONV - 1 - back:M2_CONV - back, cols] * shifted
        act_ref[:, cols] = _silu(acc)
    xbuf_ref[0:SUBLANES, :] = xbuf_ref[L:L + SUBLANES, :]

    row = lax.broadcasted_iota(jnp.int32, (L, L), 0)
    col = lax.broadcasted_iota(jnp.int32, (L, L), 1)
    tril = row >= col
    tri = jnp.where(tril, 1.0, 0.0).astype(BF16)
    dt = _softplus(dt_ref[...] + dtb_ref[...])
    adt = dt * (-jnp.exp(alog_ref[...]))
    cum = _dot_exact_left(tri, adt)
    cumt_ref[...] = cum.T
    ex = ex_ref[...]
    dtx_ref[...] = _dot_exact_right(dt, ex)
    cumx_ref[...] = _dot_exact_right(cum, ex)
    lane_half = lax.broadcasted_iota(jnp.int32, (L, LANES), 1) // M2_HEAD_DIM

    for g in range(M2_GROUPS):
        bg = act_ref[:, w + g * M2_STATE: w + (g + 1) * M2_STATE].astype(BF16)
        cg = act_ref[:, w + (M2_GROUPS + g) * M2_STATE: w + (M2_GROUPS + g + 1) * M2_STATE].astype(BF16)
        cbm = _dot_nt(cg, bg)
        heads_per_group = M2_HEADS // M2_GROUPS
        for pp in range(heads_per_group // 2):
            pair = g * (heads_per_group // 2) + pp
            sl = slice(pair * LANES, (pair + 1) * LANES)
            xs_p = act_ref[:, sl]
            xdt = xs_p * dtx_ref[:, sl]
            xdt_b = xdt.astype(BF16)
            cumx_p = cumx_ref[:, sl]
            cum_last = cumx_ref[L - 1:L, sl]
            yd = jnp.zeros((L, LANES), F32)
            for hh in range(2):
                h = 2 * pair + hh
                cc = cumx_ref[:, h * M2_HEAD_DIM:h * M2_HEAD_DIM + 1]
                cr = cumt_ref[h:h + 1, :]
                dec = jnp.exp(jnp.where(tril, cc - cr, -jnp.inf))
                yh = _dot((cbm * dec).astype(BF16), xdt_b)
                yd = jnp.where(lane_half == hh, yh, yd)
            st = st_ref[pair]
            yoff = _dot(cg, st.astype(BF16)) * jnp.exp(cumx_p)
            dte = jnp.exp(cum_last - cumx_p)
            st_ref[pair] = st * jnp.exp(cum_last) + _dot_tn(bg, (xdt * dte).astype(BF16))
            y = yd + yoff + dx_ref[:, sl] * xs_p
            y_ref[:, sl] = y * _silu(z_ref[:, sl].astype(F32))
        gs = w // M2_GROUPS
        yg = y_ref[:, g * gs:(g + 1) * gs]
        ms = jnp.mean(yg * yg, axis=-1, keepdims=True)
        o_ref[:, g * gs:(g + 1) * gs] = (yg * lax.rsqrt(ms + EPS) * ng_ref[:, g * gs:(g + 1) * gs]).astype(o_ref.dtype)


def _m2_core(proj, dt_raw, conv_w, conv_b, dt_bias, a_log, d_skip, norm_g):
    s = proj.shape[0]
    L, w = M2_CHUNK, M2_WIDTH
    pad = LANES - M2_HEADS
    dtb = jnp.pad(dt_bias, (0, pad)).reshape(1, LANES)
    alog = jnp.pad(a_log, (0, pad)).reshape(1, LANES)
    dx = jnp.repeat(d_skip, M2_HEAD_DIM).reshape(1, w)
    ex = (jnp.arange(w)[None, :] // M2_HEAD_DIM == jnp.arange(LANES)[:, None]).astype(BF16)
    full = lambda shape: pl.BlockSpec(shape, lambda c: (0,) * len(shape))
    return pl.pallas_call(
        _m2_kernel,
        out_shape=jax.ShapeDtypeStruct((s, w), BF16),
        grid=(s // L,),
        in_specs=[pl.BlockSpec((L, w), lambda c: (c, 0)),
                  pl.BlockSpec((L, w), lambda c: (c, 1)),
                  pl.BlockSpec((L, w), lambda c: (c, 2)),
                  pl.BlockSpec((L, LANES), lambda c: (c, 0)),
                  full((M2_CONV, 2 * w)), full((1, 2 * w)), full((1, LANES)), full((1, LANES)),
                  full((1, w)), full((1, w)), full((LANES, w))],
        out_specs=pl.BlockSpec((L, w), lambda c: (c, 0)),
        scratch_shapes=[pltpu.VMEM((L + SUBLANES, 2 * w), F32),
                        pltpu.VMEM((L, 2 * w), F32),
                        pltpu.VMEM((M2_HEADS // 2, M2_STATE, LANES), F32),
                        pltpu.VMEM((LANES, L), F32),
                        pltpu.VMEM((L, w), F32), pltpu.VMEM((L, w), F32), pltpu.VMEM((L, w), F32)],
        compiler_params=_params("arbitrary"),
        name="mamba2_ssd",
    )(proj, proj, proj, dt_raw, conv_w, conv_b.reshape(1, -1), dtb, alog, dx, norm_g.reshape(1, -1), ex)


def _ret_kernel(q_ref, k_ref, v_ref, g_ref, ang_ref, lg_ref, o_ref, st_ref, intra_ref, qdec_ref, kdec_ref):
    c = pl.program_id(0)
    L = RET_CHUNK
    half = RET_QK_DIM // 2
    heads = range(RET_HEADS)

    @pl.when(c == 0)
    def _():
        st_ref[...] = jnp.zeros_like(st_ref)
        row = lax.broadcasted_iota(jnp.int32, (L, L), 0)
        col = lax.broadcasted_iota(jnp.int32, (L, L), 1)
        rel = (row - col).astype(F32)
        rowf = row.astype(F32)
        for h in heads:
            lg = lg_ref[h]
            intra_ref[h] = jnp.where(rel >= 0, jnp.exp(lg * jnp.maximum(rel, 0.0)), 0.0)
            qdec_ref[h] = jnp.exp(lg * (rowf + 1.0))
            kdec_ref[h] = jnp.exp(lg * (L - 1.0 - rowf))

    pos = (c * L + lax.broadcasted_iota(jnp.int32, (L, half), 0)).astype(F32)
    ph = pos * ang_ref[...]
    cos, sin = jnp.cos(ph), jnp.sin(ph)

    def rot(x):
        x0, x1 = x[:, :half], x[:, half:]
        return jnp.concatenate([x0 * cos - x1 * sin, x1 * cos + x0 * sin], axis=1)

    qk = lambda ref, h: rot(ref[:, h * RET_QK_DIM:(h + 1) * RET_QK_DIM].astype(F32))
    vcols = [slice(h * RET_V_DIM, (h + 1) * RET_V_DIM) for h in heads]
    qbs = [qk(q_ref, h).astype(BF16) for h in heads]
    ks = [qk(k_ref, h) * (RET_QK_DIM ** -0.5) for h in heads]
    vbs = [v_ref[:, vcols[h]].astype(BF16) for h in heads]
    scores = [(_dot_nt(qbs[h], ks[h].astype(BF16)) * intra_ref[h]).astype(BF16) for h in heads]
    inters = [_dot(qbs[h], st_ref[h].astype(BF16)) for h in heads]
    for h in heads:
        kd = ks[h] * jnp.concatenate([kdec_ref[h]] * (RET_QK_DIM // LANES), axis=1)
        st_ref[h] = st_ref[h] * jnp.exp(lg_ref[h][:, 0:1] * float(L)) + _dot_tn(kd.astype(BF16), vbs[h])
    for h in heads:
        o = _dot(scores[h], vbs[h]) + jnp.concatenate([qdec_ref[h]] * (RET_V_DIM // LANES), axis=1) * inters[h]
        ms = jnp.mean(o * o, axis=-1, keepdims=True)
        o_ref[:, vcols[h]] = (o * lax.rsqrt(ms + EPS) * _silu(g_ref[:, vcols[h]].astype(F32))).astype(o_ref.dtype)


def _ret_core(proj):
    s = proj.shape[0]
    L = RET_CHUNK
    nh = RET_HEADS
    half = RET_QK_DIM // 2
    angle = (1.0 / (RET_ROT_BASE ** jnp.linspace(0.0, 1.0, half, dtype=F32))).reshape(1, half)
    log_g = jnp.log1p(-jnp.exp2(-5.0 - jnp.arange(nh, dtype=F32)))
    lg = jnp.broadcast_to(log_g[:, None, None], (nh, 1, LANES))
    return pl.pallas_call(
        _ret_kernel,
        out_shape=jax.ShapeDtypeStruct((s, RET_V_WIDTH), BF16),
        grid=(s // L,),
        in_specs=[pl.BlockSpec((L, D_MODEL), lambda c: (c, 0)),
                  pl.BlockSpec((L, D_MODEL), lambda c: (c, 1)),
                  pl.BlockSpec((L, RET_V_WIDTH), lambda c: (c, 1)),
                  pl.BlockSpec((L, RET_V_WIDTH), lambda c: (c, 2)),
                  pl.BlockSpec((1, half), lambda c: (0, 0)),
                  pl.BlockSpec((nh, 1, LANES), lambda c: (0, 0, 0))],
        out_specs=pl.BlockSpec((L, RET_V_WIDTH), lambda c: (c, 0)),
        scratch_shapes=[pltpu.VMEM((nh, RET_QK_DIM, RET_V_DIM), F32),
                        pltpu.VMEM((nh, L, L), F32), pltpu.VMEM((nh, L, LANES), F32),
                        pltpu.VMEM((nh, L, LANES), F32)],
        compiler_params=_params("arbitrary"),
        name="retention",
    )(proj, proj, proj, proj, angle, lg)


def _ret_permute_qk(w):
    d = w.shape[0]
    qk = w[:, :2 * D_MODEL].reshape(d, 2 * RET_HEADS, RET_QK_DIM // 2, 2)
    qk = jnp.swapaxes(qk, 2, 3).reshape(d, 2 * D_MODEL)
    return jnp.concatenate([qk, w[:, 2 * D_MODEL:]], axis=1)


def kernel(x, norm_g, sb_w_in, sb_w_out, hg_w_in, hg_lb_logits, hg_norm_g, hg_w_out, m2_w_in, m2_conv_w,
           m2_conv_b, m2_dt_bias, m2_a_log, m2_d, m2_norm_g, m2_w_out, ret_w_in, ret_w_out, final_g):
    b, s, d = x.shape
    depth = norm_g.shape[0]
    lb_cum = jnp.cumsum(jax.nn.softmax(hg_lb_logits.astype(F32), axis=0), axis=0)
    lower_bounds = lb_cum - lb_cum[0]
    outs = []
    for bi in range(b):
        xb = x[bi]
        for i in range(depth):
            m, j = i % 4, i // 4
            fg = final_g if i == depth - 1 else None
            if m == 0:
                proj = _norm_proj(xb, norm_g[i], sb_w_in[j].astype(BF16))
                xb = _out_proj(_sb_core(proj), sb_w_out[j].astype(BF16), xb, fg)
            elif m == 1:
                proj = _norm_proj(xb, norm_g[i], hg_w_in[j].astype(BF16))
                og = _hgrn_core(proj, lower_bounds[i], hg_norm_g[j])
                xb = _out_proj(og, hg_w_out[j].astype(BF16), xb, fg)
            elif m == 2:
                n_main = M2_WIDTH + M2_WIDTH + 2 * M2_GROUPS * M2_STATE
                w_all = m2_w_in[j].astype(BF16)
                w_dt = jnp.pad(w_all[:, n_main:], ((0, 0), (0, LANES - M2_HEADS)))
                proj, dt_raw = _m2_in_proj(xb, norm_g[i], w_all, w_dt, n_main)
                og = _m2_core(proj, dt_raw, m2_conv_w[j], m2_conv_b[j], m2_dt_bias[j], m2_a_log[j], m2_d[j],
                              m2_norm_g[j])
                xb = _out_proj(og, m2_w_out[j].astype(BF16), xb, fg)
            else:
                w_in = _ret_permute_qk(ret_w_in[j].astype(BF16))
                proj = _norm_proj(xb, norm_g[i], w_in)
                xb = _out_proj(_ret_core(proj), ret_w_out[j].astype(BF16), xb, fg)
        outs.append(xb)
    return jnp.stack(outs, axis=0)
```

```python
import functools
import math

import jax
import jax.numpy as jnp
from jax import lax
from jax.experimental import pallas as pl
from jax.experimental.pallas import tpu as pltpu

F32 = jnp.float32
BF16 = jnp.bfloat16
EPS = 1e-6
LANES = 128
SUBLANES = 8
MXU_COLS = 256
VMEM_LIMIT = 48 * 1024 * 1024

D_MODEL = 1024
SB_HEADS, SB_HEAD_DIM, SB_BLOCK = 16, 64, 128
HG_HEADS, HG_HEAD_DIM, HG_CHUNK, HG_SUB = 8, 128, 64, 16
M2_WIDTH, M2_HEADS, M2_HEAD_DIM, M2_GROUPS, M2_STATE, M2_CONV, M2_CHUNK = 2048, 32, 64, 8, 128, 4, 128
RET_HEADS, RET_QK_DIM, RET_V_DIM, RET_V_WIDTH, RET_CHUNK = 4, 256, 512, 2048, 128
RET_ROT_BASE = 10000.0
SB_UNDERFLOW = -104.0
SB_QROWS = 64
SB_WINDOW = 2
SB_TQ = 512


def _dot(a, b):
    return jnp.dot(a, b, preferred_element_type=F32)


def _dot_nt(a, b):
    return lax.dot_general(a, b, (((1,), (1,)), ((), ())), preferred_element_type=F32)


def _dot_tn(a, b):
    return lax.dot_general(a, b, (((0,), (0,)), ((), ())), preferred_element_type=F32)


def _split3(x):
    hi = x.astype(BF16)
    r1 = x - hi.astype(F32)
    mid = r1.astype(BF16)
    lo = (r1 - mid.astype(F32)).astype(BF16)
    return hi, mid, lo


def _dot_exact_right(x, m01):
    hi, mid, lo = _split3(x)
    return _dot(hi, m01) + _dot(mid, m01) + _dot(lo, m01)


def _dot_exact_left(m01, x):
    hi, mid, lo = _split3(x)
    return _dot(m01, hi) + _dot(m01, mid) + _dot(m01, lo)


def _sigmoid(x):
    return 1.0 / (1.0 + jnp.exp(-x))


def _silu(x):
    return x * _sigmoid(x)


def _softplus(x):
    u = jnp.exp(-jnp.abs(x))
    w = 1.0 + u
    log1p_u = jnp.where(w == 1.0, u, jnp.log(w) * (u / (w - 1.0)))
    return jnp.maximum(x, 0.0) + log1p_u


def _params(*sem):
    return pltpu.CompilerParams(dimension_semantics=sem, vmem_limit_bytes=VMEM_LIMIT)


def _rms_to_bf16(x_ref, g_ref, h_ref):
    xf = x_ref[...]
    ms = jnp.mean(xf * xf, axis=-1, keepdims=True)
    h_ref[...] = (xf * lax.rsqrt(ms + EPS) * g_ref[...]).astype(BF16)


def _norm_proj_kernel(x_ref, g_ref, w_ref, o_ref, h_ref):
    @pl.when(pl.program_id(1) == 0)
    def _():
        _rms_to_bf16(x_ref, g_ref, h_ref)

    o_ref[...] = _dot(h_ref[...], w_ref[...].astype(BF16)).astype(o_ref.dtype)


def _norm_proj(x, g, w, tn=2048, tm=1024):
    s, d = x.shape
    n = w.shape[1]
    tm = min(tm, s)
    return pl.pallas_call(
        _norm_proj_kernel,
        out_shape=jax.ShapeDtypeStruct((s, n), BF16),
        grid=(s // tm, n // tn),
        in_specs=[pl.BlockSpec((tm, d), lambda i, j: (i, 0)),
                  pl.BlockSpec((1, d), lambda i, j: (0, 0)),
                  pl.BlockSpec((d, tn), lambda i, j: (0, j))],
        out_specs=pl.BlockSpec((tm, tn), lambda i, j: (i, j)),
        scratch_shapes=[pltpu.VMEM((tm, d), BF16)],
        compiler_params=_params("arbitrary", "arbitrary"),
        name="norm_proj",
    )(x, g.reshape(1, d), w)


def _m2_proj_kernel(x_ref, g_ref, w_ref, dtw_ref, o_ref, dt_ref, h_ref):
    @pl.when(pl.program_id(1) == 0)
    def _():
        _rms_to_bf16(x_ref, g_ref, h_ref)
        dt_ref[...] = _dot(h_ref[...], dtw_ref[...].astype(BF16))

    o_ref[...] = _dot(h_ref[...], w_ref[...].astype(BF16)).astype(o_ref.dtype)


def _m2_in_proj(x, g, w, w_dt, n, tn=2048, tm=1024):
    s, d = x.shape
    tm = min(tm, s)
    return pl.pallas_call(
        _m2_proj_kernel,
        out_shape=(jax.ShapeDtypeStruct((s, n), BF16), jax.ShapeDtypeStruct((s, LANES), F32)),
        grid=(s // tm, n // tn),
        in_specs=[pl.BlockSpec((tm, d), lambda i, j: (i, 0)),
                  pl.BlockSpec((1, d), lambda i, j: (0, 0)),
                  pl.BlockSpec((d, tn), lambda i, j: (0, j)),
                  pl.BlockSpec((d, LANES), lambda i, j: (0, 0))],
        out_specs=(pl.BlockSpec((tm, tn), lambda i, j: (i, j)),
                   pl.BlockSpec((tm, LANES), lambda i, j: (i, 0))),
        scratch_shapes=[pltpu.VMEM((tm, d), BF16)],
        compiler_params=_params("arbitrary", "arbitrary"),
        name="m2_in_proj",
    )(x, g.reshape(1, d), w, w_dt)


def _out_proj_kernel(og_ref, w_ref, x_ref, *rest, final):
    if final:
        fg_ref, o_ref, wb_ref = rest
    else:
        o_ref, wb_ref = rest

    @pl.when(pl.program_id(0) == 0)
    def _():
        wb_ref[...] = w_ref[...].astype(BF16)

    y = x_ref[...] + _dot(og_ref[...], wb_ref[...])
    if final:
        ms = jnp.mean(y * y, axis=-1, keepdims=True)
        y = y * lax.rsqrt(ms + EPS) * fg_ref[...]
    o_ref[...] = y


def _out_proj(og, w, x, final_g=None, tm=1024):
    s, width = og.shape
    tm = min(tm, s)
    d = x.shape[1]
    in_specs = [pl.BlockSpec((tm, width), lambda i: (i, 0)),
                pl.BlockSpec((width, d), lambda i: (0, 0), pipeline_mode=pl.Buffered(1)),
                pl.BlockSpec((tm, d), lambda i: (i, 0))]
    args = [og, w, x]
    if final_g is not None:
        in_specs.append(pl.BlockSpec((1, d), lambda i: (0, 0)))
        args.append(final_g.reshape(1, d))
    return pl.pallas_call(
        functools.partial(_out_proj_kernel, final=final_g is not None),
        out_shape=jax.ShapeDtypeStruct((s, d), F32),
        grid=(s // tm,),
        in_specs=in_specs,
        out_specs=pl.BlockSpec((tm, d), lambda i: (i, 0)),
        scratch_shapes=[pltpu.VMEM((width, d), BF16)],
        compiler_params=_params("arbitrary"),
        name="out_proj",
    )(*args)


def _sb_spans(jobs, nblk, rel, k_ref, v_ref, m_rev):
    rows = jobs[0][0].shape[0]
    width = nblk * SB_BLOCK
    blk = lambda x, b: x[:, b * SB_BLOCK:(b + 1) * SB_BLOCK]
    starts = [pl.multiple_of(ks, SB_QROWS) for _, ks, _, _ in jobs]
    zs = [_dot_nt(qs, k_ref[pl.ds(ks, width), :]) for (qs, _, _, _), ks in zip(jobs, starts)]
    masks, l1ms = [], []
    for (_, _, bound, _), z in zip(jobs, zs):
        mask = [rel < (bound - b * SB_BLOCK) for b in range(nblk)]
        l1m = [jnp.where(mask[b], -(jnp.maximum(blk(z, b), 0.0) + jnp.log(1.0 + jnp.exp(-jnp.abs(blk(z, b))))), 0.0)
               for b in range(nblk)]
        masks.append(mask)
        l1ms.append(l1m)
    revs = []
    for l1m in l1ms:
        stacked = jnp.concatenate(l1m, axis=0)
        hi = stacked.astype(BF16)
        lo = (stacked - hi.astype(F32)).astype(BF16)
        revs.append(_dot(hi, m_rev) + _dot(lo, m_rev))
    probs, cs = [], []
    for n, ((_, _, _, c), z) in enumerate(zip(jobs, zs)):
        parts = [None] * nblk
        for b in reversed(range(nblk)):
            e = jnp.exp(blk(z, b) + revs[n][b * rows:(b + 1) * rows] + c)
            parts[b] = jnp.where(masks[n][b], e, 0.0).astype(BF16)
            c = c + jnp.sum(l1ms[n][b], axis=1, keepdims=True)
        probs.append(jnp.concatenate(parts, axis=1))
        cs.append(c)
    return [(_dot(a, v_ref[pl.ds(ks, width), :]), c) for a, ks, c in zip(probs, starts, cs)]


def _sb_kernel(q_ref, k_ref, v_ref, g_ref, o_ref, acc_ref, c_ref):
    i = pl.program_id(1)
    njobs = q_ref.shape[0] // SB_QROWS
    nkeys = k_ref.shape[0]
    n_half = LANES // SB_HEAD_DIM
    rows = n_half * SB_QROWS
    span = SB_WINDOW * SB_BLOCK
    row = lax.broadcasted_iota(jnp.int32, (SB_BLOCK, SB_BLOCK), 0)
    col = lax.broadcasted_iota(jnp.int32, (SB_BLOCK, SB_BLOCK), 1)
    m_rev = jnp.where(row >= col, 1.0, 0.0).astype(BF16)
    lane_q = lax.broadcasted_iota(jnp.int32, (SB_QROWS, LANES), 1)
    hmasks = [(lane_q >= h * SB_HEAD_DIM) & (lane_q < (h + 1) * SB_HEAD_DIM) for h in range(n_half)]
    lane = lax.broadcasted_iota(jnp.int32, (rows, SB_BLOCK), 1)
    qoff = lax.broadcasted_iota(jnp.int32, (rows, SB_BLOCK), 0) & (SB_QROWS - 1)
    jobs = []
    for n in range(njobs):
        q = q_ref[n * SB_QROWS:(n + 1) * SB_QROWS, :] * (SB_HEAD_DIM ** -0.5)
        qs = jnp.concatenate([jnp.where(m, q, jnp.zeros_like(q)) for m in hmasks], axis=0)
        q0 = (i * njobs + n) * SB_QROWS
        ks = jnp.clip(q0 + SB_QROWS - span, 0, nkeys - span)
        jobs.append((qs, ks, q0 - ks, jnp.zeros((rows, 1), F32)))
    results = _sb_spans(jobs, SB_WINDOW, lane - qoff, k_ref, v_ref, m_rev)
    gate = g_ref[...].astype(F32)
    for n, (pv, c) in enumerate(results):
        acc_ref[n] = pv
        c_ref[n] = jnp.broadcast_to(c, c_ref.shape[1:])

    def cond(st):
        ends, cmaxes = st[:njobs], st[njobs:]
        return functools.reduce(jnp.logical_or, [jnp.logical_and(e > 0, cm > SB_UNDERFLOW)
                                                 for e, cm in zip(ends, cmaxes)])

    def body(st):
        ends = st[:njobs]
        starts = [jnp.maximum(e - SB_BLOCK, 0) for e in ends]
        step = [(qs, s0, e - s0, c_ref[n]) for n, ((qs, _, _, _), s0, e) in enumerate(zip(jobs, starts, ends))]
        new = []
        for n, (pv, cn) in enumerate(_sb_spans(step, 1, lane, k_ref, v_ref, m_rev)):
            acc_ref[n] += pv
            c_ref[n] = cn
            new.append(jnp.max(cn))
        return (*starts, *new)

    lax.while_loop(cond, body, (*[ks for _, ks, _, _ in jobs], *[jnp.max(c) for _, c in results]))
    for n in range(njobs):
        out = acc_ref[n, 0:SB_QROWS, :]
        for h in range(1, n_half):
            out = jnp.where(hmasks[h], acc_ref[n, h * SB_QROWS:(h + 1) * SB_QROWS, :], out)
        sl = slice(n * SB_QROWS, (n + 1) * SB_QROWS)
        o_ref[sl, :] = (out * _silu(gate[sl])).astype(o_ref.dtype)


def _sb_core(proj):
    s = proj.shape[0]
    nb = D_MODEL // LANES
    tq = SB_TQ
    return pl.pallas_call(
        _sb_kernel,
        out_shape=jax.ShapeDtypeStruct((s, D_MODEL), BF16),
        grid=(nb, s // tq),
        in_specs=[pl.BlockSpec((tq, LANES), lambda p, i: (i, p)),
                  pl.BlockSpec((s, LANES), lambda p, i: (0, nb + p)),
                  pl.BlockSpec((s, LANES), lambda p, i: (0, 2 * nb + p)),
                  pl.BlockSpec((tq, LANES), lambda p, i: (i, 3 * nb + p))],
        out_specs=pl.BlockSpec((tq, LANES), lambda p, i: (i, p)),
        scratch_shapes=[pltpu.VMEM((tq // SB_QROWS, LANES // SB_HEAD_DIM * SB_QROWS, LANES), F32)] * 2,
        compiler_params=_params("arbitrary", "arbitrary"),
        name="sb_attention",
    )(proj, proj, proj, proj)


def _hgrn_kernel(q_ref, f_ref, i_ref, g_ref, lb_ref, ng_ref, o_ref, st_ref, gc_ref, kc_ref, dg_ref, *, chunk):
    @pl.when(pl.program_id(1) == 0)
    def _():
        st_ref[...] = jnp.zeros_like(st_ref)

    c = chunk
    sub = HG_SUB
    nsub = c // sub
    row = lax.broadcasted_iota(jnp.int32, (c, c), 0)
    col = lax.broadcasted_iota(jnp.int32, (c, c), 1)
    tri = jnp.where(row >= col, 1.0, 0.0).astype(BF16)
    krow = lax.broadcasted_iota(jnp.int32, (c, LANES), 0)
    lb = lb_ref[...]
    nchunk = q_ref.shape[0] // c
    chunks = [slice(ci * c, (ci + 1) * c) for ci in range(nchunk)]

    q = _silu(q_ref[...].astype(F32))
    f = lb + (1.0 - lb) * _sigmoid(f_ref[...].astype(F32))
    k = 1.0 - f
    vb = i_ref[...].astype(BF16)
    logf = jnp.log2(f)
    g = jnp.concatenate([_dot_exact_left(tri, logf[ch]) for ch in chunks], axis=0)
    gc_ref[...] = g
    kc_ref[...] = k
    dg_ref[...] = jnp.zeros_like(dg_ref)
    crosses = []
    for ch in chunks:
        per_sub = [jnp.zeros((sub, c), F32)]
        for si in range(1, nsub):
            lo = ch.start + si * sub
            ref = gc_ref[lo - 1:lo, :]
            kin = jnp.where(krow < si * sub, k[ch] * jnp.exp2(ref - g[ch]), 0.0)
            qin = q[lo:lo + sub, :] * jnp.exp2(g[lo:lo + sub, :] - ref)
            per_sub.append(_dot_nt(qin.astype(BF16), kin.astype(BF16)))
        crosses.append(per_sub)
    o_intra = []
    for ch, per_sub in zip(chunks, crosses):
        for si in range(nsub):
            for rb in range(sub // SUBLANES):
                r0 = ch.start + si * sub + rb * SUBLANES
                gb, qb = g[r0:r0 + SUBLANES, :], q[r0:r0 + SUBLANES, :]
                for s in range(si * sub, si * sub + (rb + 1) * SUBLANES):
                    e = jnp.exp2(gb - gc_ref[ch.start + s:ch.start + s + 1, :])
                    dg_ref[r0:r0 + SUBLANES, s:s + 1] = jnp.sum(
                        qb * e * kc_ref[ch.start + s:ch.start + s + 1, :], axis=1, keepdims=True)
        sc = jnp.where(row >= col, jnp.where(row // sub == col // sub, dg_ref[ch, :],
                                             jnp.concatenate(per_sub, axis=0)), 0.0)
        o_intra.append(_dot(sc.astype(BF16), vb[ch]))
    qd = (q * jnp.exp2(g)).astype(BF16)
    updates, decays = [], []
    for ch in chunks:
        glast = gc_ref[ch.stop - 1:ch.stop, :]
        updates.append(_dot_tn(vb[ch], (k[ch] * jnp.exp2(glast - g[ch])).astype(BF16)))
        decays.append(jnp.exp2(glast))
    st = st_ref[...]
    outs = []
    for ci, ch in enumerate(chunks):
        outs.append(o_intra[ci] + _dot_nt(qd[ch], st.astype(BF16)))
        st = st * decays[ci] + updates[ci]
    st_ref[...] = st
    o = jnp.concatenate(outs, axis=0)
    ms = jnp.mean(o * o, axis=-1, keepdims=True)
    o_ref[...] = (o * lax.rsqrt(ms + EPS) * ng_ref[...] * _silu(g_ref[...].astype(F32))).astype(o_ref.dtype)


def _hgrn_core(proj, lb, norm_g, rows=512, chunk=HG_CHUNK):
    s = proj.shape[0]
    rows = min(rows, s)
    nh = HG_HEADS
    blk = lambda off: pl.BlockSpec((rows, LANES), lambda h, t: (t, off * nh + h))
    vec = pl.BlockSpec((1, LANES), lambda h, t: (0, h))
    return pl.pallas_call(
        functools.partial(_hgrn_kernel, chunk=chunk),
        out_shape=jax.ShapeDtypeStruct((s, D_MODEL), BF16),
        grid=(nh, s // rows),
        in_specs=[blk(0), blk(1), blk(2), blk(3), vec, vec],
        out_specs=pl.BlockSpec((rows, LANES), lambda h, t: (t, h)),
        scratch_shapes=[pltpu.VMEM((HG_HEAD_DIM, HG_HEAD_DIM), F32),
                        pltpu.VMEM((rows, LANES), F32), pltpu.VMEM((rows, LANES), F32),
                        pltpu.VMEM((rows, chunk), F32)],
        compiler_params=_params("arbitrary", "arbitrary"),
        name="hgrn2",
    )(proj, proj, proj, proj, lb.reshape(1, -1), norm_g.reshape(1, -1))


def _m2_kernel(z_ref, xs_ref, bc_ref, dt_ref, cw_ref, cb_ref, dtb_ref, alog_ref, dx_ref, ng_ref, ex_ref,
               o_ref, xbuf_ref, act_ref, st_ref, cumt_ref, dtx_ref, cumx_ref, y_ref):
    L = M2_CHUNK
    w = M2_WIDTH
    slab = 512

    @pl.when(pl.program_id(0) == 0)
    def _():
        xbuf_ref[0:SUBLANES, :] = jnp.zeros((SUBLANES, 2 * w), F32)
        st_ref[...] = jnp.zeros_like(st_ref)

    xbuf_ref[SUBLANES:SUBLANES + L, 0:w] = xs_ref[...].astype(F32)
    xbuf_ref[SUBLANES:SUBLANES + L, w:2 * w] = bc_ref[...].astype(F32)
    sub = lax.broadcasted_iota(jnp.int32, (L // SUBLANES, SUBLANES, slab), 1)
    for c0 in range(0, 2 * w, slab):
        cols = slice(c0, c0 + slab)
        window = xbuf_ref[:, cols].reshape(L // SUBLANES + 1, SUBLANES, slab)
        acc = cb_ref[:, cols] + cw_ref[M2_CONV - 1:M2_CONV, cols] * xbuf_ref[SUBLANES:, cols]
        for back in range(1, M2_CONV):
            rot = pltpu.roll(window, back, axis=1)
            shifted = jnp.where(sub < back, rot[:-1], rot[1:]).reshape(L, slab)
            acc = acc + cw_ref[M2_CONV - 1 - back:M2_CONV - back, cols] * shifted
        act_ref[:, cols] = _silu(acc)
    xbuf_ref[0:SUBLANES, :] = xbuf_ref[L:L + SUBLANES, :]

    row = lax.broadcasted_iota(jnp.int32, (L, L), 0)
    col = lax.broadcasted_iota(jnp.int32, (L, L), 1)
    tril = row >= col
    tri = jnp.where(tril, 1.0, 0.0).astype(BF16)
    dt = _softplus(dt_ref[...] + dtb_ref[...])
    adt = dt * (-jnp.exp(alog_ref[...]))
    cum = _dot_exact_left(tri, adt)
    cumt_ref[...] = cum.T
    ex = ex_ref[...]
    dtx_ref[...] = _dot_exact_right(dt, ex)
    cumx_ref[...] = _dot_exact_right(cum, ex)
    lane_half = lax.broadcasted_iota(jnp.int32, (L, LANES), 1) // M2_HEAD_DIM

    for g in range(M2_GROUPS):
        bg = act_ref[:, w + g * M2_STATE: w + (g + 1) * M2_STATE].astype(BF16)
        cg = act_ref[:, w + (M2_GROUPS + g) * M2_STATE: w + (M2_GROUPS + g + 1) * M2_STATE].astype(BF16)
        cbm = _dot_nt(cg, bg)
        heads_per_group = M2_HEADS // M2_GROUPS
        for pp in range(heads_per_group // 2):
            pair = g * (heads_per_group // 2) + pp
            sl = slice(pair * LANES, (pair + 1) * LANES)
            xs_p = act_ref[:, sl]
            xdt = xs_p * dtx_ref[:, sl]
            xdt_b = xdt.astype(BF16)
            cumx_p = cumx_ref[:, sl]
            cum_last = cumx_ref[L - 1:L, sl]
            yd = jnp.zeros((L, LANES), F32)
            for hh in range(2):
                h = 2 * pair + hh
                cc = cumx_ref[:, h * M2_HEAD_DIM:h * M2_HEAD_DIM + 1]
                cr = cumt_ref[h:h + 1, :]
                dec = jnp.exp(jnp.where(tril, cc - cr, -jnp.inf))
                yh = _dot((cbm * dec).astype(BF16), xdt_b)
                yd = jnp.where(lane_half == hh, yh, yd)
            st = st_ref[pair]
            yoff = _dot(cg, st.astype(BF16)) * jnp.exp(cumx_p)
            dte = jnp.exp(cum_last - cumx_p)
            st_ref[pair] = st * jnp.exp(cum_last) + _dot_tn(bg, (xdt * dte).astype(BF16))
            y = yd + yoff + dx_ref[:, sl] * xs_p
            y_ref[:, sl] = y * _silu(z_ref[:, sl].astype(F32))
        gs = w // M2_GROUPS
        yg = y_ref[:, g * gs:(g + 1) * gs]
        ms = jnp.mean(yg * yg, axis=-1, keepdims=True)
        o_ref[:, g * gs:(g + 1) * gs] = (yg * lax.rsqrt(ms + EPS) * ng_ref[:, g * gs:(g + 1) * gs]).astype(o_ref.dtype)


def _m2_core(proj, dt_raw, conv_w, conv_b, dt_bias, a_log, d_skip, norm_g):
    s = proj.shape[0]
    L, w = M2_CHUNK, M2_WIDTH
    pad = LANES - M2_HEADS
    dtb = jnp.pad(dt_bias, (0, pad)).reshape(1, LANES)
    alog = jnp.pad(a_log, (0, pad)).reshape(1, LANES)
    dx = jnp.repeat(d_skip, M2_HEAD_DIM).reshape(1, w)
    ex = (jnp.arange(w)[None, :] // M2_HEAD_DIM == jnp.arange(LANES)[:, None]).astype(BF16)
    full = lambda shape: pl.BlockSpec(shape, lambda c: (0,) * len(shape))
    return pl.pallas_call(
        _m2_kernel,
        out_shape=jax.ShapeDtypeStruct((s, w), BF16),
        grid=(s // L,),
        in_specs=[pl.BlockSpec((L, w), lambda c: (c, 0)),
                  pl.BlockSpec((L, w), lambda c: (c, 1)),
                  pl.BlockSpec((L, w), lambda c: (c, 2)),
                  pl.BlockSpec((L, LANES), lambda c: (c, 0)),
                  full((M2_CONV, 2 * w)), full((1, 2 * w)), full((1, LANES)), full((1, LANES)),
                  full((1, w)), full((1, w)), full((LANES, w))],
        out_specs=pl.BlockSpec((L, w), lambda c: (c, 0)),
        scratch_shapes=[pltpu.VMEM((L + SUBLANES, 2 * w), F32),
                        pltpu.VMEM((L, 2 * w), F32),
                        pltpu.VMEM((M2_HEADS // 2, M2_STATE, LANES), F32),
                        pltpu.VMEM((LANES, L), F32),
                        pltpu.VMEM((L, w), F32), pltpu.VMEM((L, w), F32), pltpu.VMEM((L, w), F32)],
        compiler_params=_params("arbitrary"),
        name="mamba2_ssd",
    )(proj, proj, proj, dt_raw, conv_w, conv_b.reshape(1, -1), dtb, alog, dx, norm_g.reshape(1, -1), ex)


def _ret_kernel(q_ref, k_ref, v_ref, g_ref, ang_ref, lg_ref, o_ref, st_ref, intra_ref, qdec_ref, kdec_ref,
                cos_ref, sin_ref):
    c = pl.program_id(0)
    L = RET_CHUNK
    half = RET_QK_DIM // 2
    heads = range(RET_HEADS)

    @pl.when(c == 0)
    def _():
        st_ref[...] = jnp.zeros_like(st_ref)
        row = lax.broadcasted_iota(jnp.int32, (L, L), 0)
        col = lax.broadcasted_iota(jnp.int32, (L, L), 1)
        rel = (row - col).astype(F32)
        rowf = row.astype(F32)
        for h in heads:
            lg = lg_ref[h]
            intra_ref[h] = jnp.where(rel >= 0, jnp.exp(lg * jnp.maximum(rel, 0.0)), 0.0)
            qdec_ref[h] = jnp.exp(lg * (rowf + 1.0))
            kdec_ref[h] = jnp.exp(lg * (L - 1.0 - rowf))
        ph = lax.broadcasted_iota(jnp.int32, (L, half), 0).astype(F32) * ang_ref[...]
        cos_ref[...] = jnp.cos(ph)
        sin_ref[...] = jnp.sin(ph)

    @pl.when(c > 0)
    def _():
        step = float(L) * ang_ref[...]
        cd, sd = jnp.cos(step), jnp.sin(step)
        c0, s0 = cos_ref[...], sin_ref[...]
        cos_ref[...] = c0 * cd - s0 * sd
        sin_ref[...] = s0 * cd + c0 * sd

    cos, sin = cos_ref[...], sin_ref[...]

    def rot(x):
        x0, x1 = x[:, :half], x[:, half:]
        return jnp.concatenate([x0 * cos - x1 * sin, x1 * cos + x0 * sin], axis=1)

    qk = lambda ref, h: rot(ref[:, h * RET_QK_DIM:(h + 1) * RET_QK_DIM].astype(F32))
    vcols = [slice(h * RET_V_DIM, (h + 1) * RET_V_DIM) for h in heads]
    qbs = [qk(q_ref, h).astype(BF16) for h in heads]
    ks = [qk(k_ref, h) * (RET_QK_DIM ** -0.5) for h in heads]
    vbs = [v_ref[:, vcols[h]].astype(BF16) for h in heads]
    scores = [(_dot_nt(qbs[h], ks[h].astype(BF16)) * intra_ref[h]).astype(BF16) for h in heads]
    inters = [_dot(qbs[h], st_ref[h].astype(BF16)) for h in heads]
    for h in heads:
        kd = ks[h] * jnp.concatenate([kdec_ref[h]] * (RET_QK_DIM // LANES), axis=1)
        st_ref[h] = st_ref[h] * jnp.exp(lg_ref[h][:, 0:1] * float(L)) + _dot_tn(kd.astype(BF16), vbs[h])
    for h in heads:
        o = _dot(scores[h], vbs[h]) + jnp.concatenate([qdec_ref[h]] * (RET_V_DIM // LANES), axis=1) * inters[h]
        ms = jnp.mean(o * o, axis=-1, keepdims=True)
        o_ref[:, vcols[h]] = (o * lax.rsqrt(ms + EPS) * _silu(g_ref[:, vcols[h]].astype(F32))).astype(o_ref.dtype)


def _ret_core(proj):
    s = proj.shape[0]
    L = RET_CHUNK
    nh = RET_HEADS
    half = RET_QK_DIM // 2
    angle = (1.0 / (RET_ROT_BASE ** jnp.linspace(0.0, 1.0, half, dtype=F32))).reshape(1, half)
    log_g = jnp.log1p(-jnp.exp2(-5.0 - jnp.arange(nh, dtype=F32)))
    lg = jnp.broadcast_to(log_g[:, None, None], (nh, 1, LANES))
    return pl.pallas_call(
        _ret_kernel,
        out_shape=jax.ShapeDtypeStruct((s, RET_V_WIDTH), BF16),
        grid=(s // L,),
        in_specs=[pl.BlockSpec((L, D_MODEL), lambda c: (c, 0)),
                  pl.BlockSpec((L, D_MODEL), lambda c: (c, 1)),
                  pl.BlockSpec((L, RET_V_WIDTH), lambda c: (c, 1)),
                  pl.BlockSpec((L, RET_V_WIDTH), lambda c: (c, 2)),
                  pl.BlockSpec((1, half), lambda c: (0, 0)),
                  pl.BlockSpec((nh, 1, LANES), lambda c: (0, 0, 0))],
        out_specs=pl.BlockSpec((L, RET_V_WIDTH), lambda c: (c, 0)),
        scratch_shapes=[pltpu.VMEM((nh, RET_QK_DIM, RET_V_DIM), F32),
                        pltpu.VMEM((nh, L, L), F32), pltpu.VMEM((nh, L, LANES), F32),
                        pltpu.VMEM((nh, L, LANES), F32),
                        pltpu.VMEM((L, half), F32), pltpu.VMEM((L, half), F32)],
        compiler_params=_params("arbitrary"),
        name="retention",
    )(proj, proj, proj, proj, angle, lg)


def _ret_permute_qk(w):
    d = w.shape[0]
    qk = w[:, :2 * D_MODEL].reshape(d, 2 * RET_HEADS, RET_QK_DIM // 2, 2)
    qk = jnp.swapaxes(qk, 2, 3).reshape(d, 2 * D_MODEL)
    return jnp.concatenate([qk, w[:, 2 * D_MODEL:]], axis=1)


def kernel(x, norm_g, sb_w_in, sb_w_out, hg_w_in, hg_lb_logits, hg_norm_g, hg_w_out, m2_w_in, m2_conv_w,
           m2_conv_b, m2_dt_bias, m2_a_log, m2_d, m2_norm_g, m2_w_out, ret_w_in, ret_w_out, final_g):
    b, s, d = x.shape
    depth = norm_g.shape[0]
    lb_cum = jnp.cumsum(jax.nn.softmax(hg_lb_logits.astype(F32), axis=0), axis=0)
    lower_bounds = lb_cum - lb_cum[0]
    outs = []
    for bi in range(b):
        xb = x[bi]
        for i in range(depth):
            m, j = i % 4, i // 4
            fg = final_g if i == depth - 1 else None
            if m == 0:
                proj = _norm_proj(xb, norm_g[i], sb_w_in[j])
                xb = _out_proj(_sb_core(proj), sb_w_out[j], xb, fg)
            elif m == 1:
                proj = _norm_proj(xb, norm_g[i], hg_w_in[j])
                og = _hgrn_core(proj, lower_bounds[i], hg_norm_g[j])
                xb = _out_proj(og, hg_w_out[j], xb, fg)
            elif m == 2:
                n_main = M2_WIDTH + M2_WIDTH + 2 * M2_GROUPS * M2_STATE
                w_dt = jnp.pad(m2_w_in[j][:, n_main:], ((0, 0), (0, LANES - M2_HEADS)))
                proj, dt_raw = _m2_in_proj(xb, norm_g[i], m2_w_in[j], w_dt, n_main)
                og = _m2_core(proj, dt_raw, m2_conv_w[j], m2_conv_b[j], m2_dt_bias[j], m2_a_log[j], m2_d[j],
                              m2_norm_g[j])
                xb = _out_proj(og, m2_w_out[j], xb, fg)
            else:
                w_in = _ret_permute_qk(ret_w_in[j].astype(BF16))
                proj = _norm_proj(xb, norm_g[i], w_in)
                xb = _out_proj(_ret_core(proj), ret_w_out[j], xb, fg)
        outs.append(xb)
    return outs[0][None] if b == 1 else jnp.stack(outs, axis=0)
```

```python
import functools
import math

import jax
import jax.numpy as jnp
from jax import lax
from jax.experimental import pallas as pl
from jax.experimental.pallas import tpu as pltpu

F32 = jnp.float32
BF16 = jnp.bfloat16
EPS = 1e-6
LANES = 128
SUBLANES = 8
MXU_COLS = 256
VMEM_LIMIT = 48 * 1024 * 1024

D_MODEL = 1024
SB_HEADS, SB_HEAD_DIM, SB_BLOCK = 16, 64, 128
HG_HEADS, HG_HEAD_DIM, HG_CHUNK, HG_SUB = 8, 128, 64, 16
M2_WIDTH, M2_HEADS, M2_HEAD_DIM, M2_GROUPS, M2_STATE, M2_CONV, M2_CHUNK = 2048, 32, 64, 8, 128, 4, 128
RET_HEADS, RET_QK_DIM, RET_V_DIM, RET_V_WIDTH, RET_CHUNK = 4, 256, 512, 2048, 128
RET_ROT_BASE = 10000.0
SB_UNDERFLOW = -104.0
SB_QROWS = 64
SB_WINDOW = 2
SB_TQ = 512


def _dot(a, b):
    return jnp.dot(a, b, preferred_element_type=F32)


def _dot_nt(a, b):
    return lax.dot_general(a, b, (((1,), (1,)), ((), ())), preferred_element_type=F32)


def _dot_tn(a, b):
    return lax.dot_general(a, b, (((0,), (0,)), ((), ())), preferred_element_type=F32)


def _split3(x):
    hi = x.astype(BF16)
    r1 = x - hi.astype(F32)
    mid = r1.astype(BF16)
    lo = (r1 - mid.astype(F32)).astype(BF16)
    return hi, mid, lo


def _dot_exact_right(x, m01):
    hi, mid, lo = _split3(x)
    return _dot(hi, m01) + _dot(mid, m01) + _dot(lo, m01)


def _dot_exact_left(m01, x):
    hi, mid, lo = _split3(x)
    return _dot(m01, hi) + _dot(m01, mid) + _dot(m01, lo)


def _sigmoid(x):
    return 1.0 / (1.0 + jnp.exp(-x))


def _silu(x):
    return x * _sigmoid(x)


def _softplus(x):
    u = jnp.exp(-jnp.abs(x))
    w = 1.0 + u
    log1p_u = jnp.where(w == 1.0, u, jnp.log(w) * (u / (w - 1.0)))
    return jnp.maximum(x, 0.0) + log1p_u


def _params(*sem):
    return pltpu.CompilerParams(dimension_semantics=sem, vmem_limit_bytes=VMEM_LIMIT)


def _rms_to_bf16(x_ref, g_ref, h_ref):
    xf = x_ref[...]
    ms = jnp.mean(xf * xf, axis=-1, keepdims=True)
    h_ref[...] = (xf * lax.rsqrt(ms + EPS) * g_ref[...]).astype(BF16)


def _norm_proj_kernel(x_ref, g_ref, w_ref, o_ref, h_ref):
    @pl.when(pl.program_id(1) == 0)
    def _():
        _rms_to_bf16(x_ref, g_ref, h_ref)

    o_ref[...] = _dot(h_ref[...], w_ref[...].astype(BF16)).astype(o_ref.dtype)


def _norm_proj(x, g, w, tn=2048, tm=1024):
    s, d = x.shape
    n = w.shape[1]
    tm = min(tm, s)
    return pl.pallas_call(
        _norm_proj_kernel,
        out_shape=jax.ShapeDtypeStruct((s, n), BF16),
        grid=(s // tm, n // tn),
        in_specs=[pl.BlockSpec((tm, d), lambda i, j: (i, 0)),
                  pl.BlockSpec((1, d), lambda i, j: (0, 0)),
                  pl.BlockSpec((d, tn), lambda i, j: (0, j))],
        out_specs=pl.BlockSpec((tm, tn), lambda i, j: (i, j)),
        scratch_shapes=[pltpu.VMEM((tm, d), BF16)],
        compiler_params=_params("arbitrary", "arbitrary"),
        name="norm_proj",
    )(x, g.reshape(1, d), w)


def _m2_proj_kernel(x_ref, g_ref, w_ref, dtw_ref, o_ref, dt_ref, h_ref):
    @pl.when(pl.program_id(1) == 0)
    def _():
        _rms_to_bf16(x_ref, g_ref, h_ref)
        dt_ref[...] = _dot(h_ref[...], dtw_ref[...].astype(BF16))

    o_ref[...] = _dot(h_ref[...], w_ref[...].astype(BF16)).astype(o_ref.dtype)


def _m2_in_proj(x, g, w, w_dt, n, tn=2048, tm=1024):
    s, d = x.shape
    tm = min(tm, s)
    return pl.pallas_call(
        _m2_proj_kernel,
        out_shape=(jax.ShapeDtypeStruct((s, n), BF16), jax.ShapeDtypeStruct((s, LANES), F32)),
        grid=(s // tm, n // tn),
        in_specs=[pl.BlockSpec((tm, d), lambda i, j: (i, 0)),
                  pl.BlockSpec((1, d), lambda i, j: (0, 0)),
                  pl.BlockSpec((d, tn), lambda i, j: (0, j)),
                  pl.BlockSpec((d, LANES), lambda i, j: (0, 0))],
        out_specs=(pl.BlockSpec((tm, tn), lambda i, j: (i, j)),
                   pl.BlockSpec((tm, LANES), lambda i, j: (i, 0))),
        scratch_shapes=[pltpu.VMEM((tm, d), BF16)],
        compiler_params=_params("arbitrary", "arbitrary"),
        name="m2_in_proj",
    )(x, g.reshape(1, d), w, w_dt)


def _out_proj_kernel(og_ref, w_ref, x_ref, *rest, final):
    if final:
        fg_ref, o_ref, wb_ref = rest
    else:
        o_ref, wb_ref = rest

    @pl.when(pl.program_id(0) == 0)
    def _():
        wb_ref[...] = w_ref[...].astype(BF16)

    y = x_ref[...] + _dot(og_ref[...], wb_ref[...])
    if final:
        ms = jnp.mean(y * y, axis=-1, keepdims=True)
        y = y * lax.rsqrt(ms + EPS) * fg_ref[...]
    o_ref[...] = y


def _out_proj(og, w, x, final_g=None, tm=1024):
    s, width = og.shape
    tm = min(tm, s)
    d = x.shape[1]
    in_specs = [pl.BlockSpec((tm, width), lambda i: (i, 0)),
                pl.BlockSpec((width, d), lambda i: (0, 0), pipeline_mode=pl.Buffered(1)),
                pl.BlockSpec((tm, d), lambda i: (i, 0))]
    args = [og, w, x]
    if final_g is not None:
        in_specs.append(pl.BlockSpec((1, d), lambda i: (0, 0)))
        args.append(final_g.reshape(1, d))
    return pl.pallas_call(
        functools.partial(_out_proj_kernel, final=final_g is not None),
        out_shape=jax.ShapeDtypeStruct((s, d), F32),
        grid=(s // tm,),
        in_specs=in_specs,
        out_specs=pl.BlockSpec((tm, d), lambda i: (i, 0)),
        scratch_shapes=[pltpu.VMEM((width, d), BF16)],
        compiler_params=_params("arbitrary"),
        name="out_proj",
    )(*args)


def _sb_spans(jobs, nblk, rel, k_ref, v_ref, m_rev):
    rows = jobs[0][0].shape[0]
    width = nblk * SB_BLOCK
    blk = lambda x, b: x[:, b * SB_BLOCK:(b + 1) * SB_BLOCK]
    starts = [pl.multiple_of(ks, SB_QROWS) for _, ks, _, _ in jobs]
    zs = [_dot_nt(qs, k_ref[pl.ds(ks, width), :]) for (qs, _, _, _), ks in zip(jobs, starts)]
    masks, l1ms = [], []
    for (_, _, bound, _), z in zip(jobs, zs):
        mask = [rel < (bound - b * SB_BLOCK) for b in range(nblk)]
        l1m = [jnp.where(mask[b], -(jnp.maximum(blk(z, b), 0.0) + jnp.log(1.0 + jnp.exp(-jnp.abs(blk(z, b))))), 0.0)
               for b in range(nblk)]
        masks.append(mask)
        l1ms.append(l1m)
    revs = []
    for l1m in l1ms:
        stacked = jnp.concatenate(l1m, axis=0)
        hi = stacked.astype(BF16)
        lo = (stacked - hi.astype(F32)).astype(BF16)
        revs.append(_dot(hi, m_rev) + _dot(lo, m_rev))
    probs, cs = [], []
    for n, ((_, _, _, c), z) in enumerate(zip(jobs, zs)):
        parts = [None] * nblk
        for b in reversed(range(nblk)):
            e = jnp.exp(blk(z, b) + revs[n][b * rows:(b + 1) * rows] + c)
            parts[b] = jnp.where(masks[n][b], e, 0.0).astype(BF16)
            c = c + jnp.sum(l1ms[n][b], axis=1, keepdims=True)
        probs.append(jnp.concatenate(parts, axis=1))
        cs.append(c)
    return [(_dot(a, v_ref[pl.ds(ks, width), :]), c) for a, ks, c in zip(probs, starts, cs)]


def _sb_kernel(q_ref, k_ref, v_ref, g_ref, o_ref, acc_ref, c_ref):
    i = pl.program_id(1)
    njobs = q_ref.shape[0] // SB_QROWS
    nkeys = k_ref.shape[0]
    n_half = LANES // SB_HEAD_DIM
    rows = n_half * SB_QROWS
    span = SB_WINDOW * SB_BLOCK
    row = lax.broadcasted_iota(jnp.int32, (SB_BLOCK, SB_BLOCK), 0)
    col = lax.broadcasted_iota(jnp.int32, (SB_BLOCK, SB_BLOCK), 1)
    m_rev = jnp.where(row >= col, 1.0, 0.0).astype(BF16)
    lane_q = lax.broadcasted_iota(jnp.int32, (SB_QROWS, LANES), 1)
    hmasks = [(lane_q >= h * SB_HEAD_DIM) & (lane_q < (h + 1) * SB_HEAD_DIM) for h in range(n_half)]
    lane = lax.broadcasted_iota(jnp.int32, (rows, SB_BLOCK), 1)
    qoff = lax.broadcasted_iota(jnp.int32, (rows, SB_BLOCK), 0) & (SB_QROWS - 1)
    jobs = []
    for n in range(njobs):
        q = q_ref[n * SB_QROWS:(n + 1) * SB_QROWS, :] * (SB_HEAD_DIM ** -0.5)
        qs = jnp.concatenate([jnp.where(m, q, jnp.zeros_like(q)) for m in hmasks], axis=0)
        q0 = (i * njobs + n) * SB_QROWS
        ks = jnp.clip(q0 + SB_QROWS - span, 0, nkeys - span)
        jobs.append((qs, ks, q0 - ks, jnp.zeros((rows, 1), F32)))
    results = _sb_spans(jobs, SB_WINDOW, lane - qoff, k_ref, v_ref, m_rev)
    gate = g_ref[...].astype(F32)
    for n, (pv, c) in enumerate(results):
        acc_ref[n] = pv
        c_ref[n] = jnp.broadcast_to(c, c_ref.shape[1:])

    def cond(st):
        ends, cmaxes = st[:njobs], st[njobs:]
        return functools.reduce(jnp.logical_or, [jnp.logical_and(e > 0, cm > SB_UNDERFLOW)
                                                 for e, cm in zip(ends, cmaxes)])

    def body(st):
        ends = st[:njobs]
        starts = [jnp.maximum(e - SB_BLOCK, 0) for e in ends]
        step = [(qs, s0, e - s0, c_ref[n]) for n, ((qs, _, _, _), s0, e) in enumerate(zip(jobs, starts, ends))]
        new = []
        for n, (pv, cn) in enumerate(_sb_spans(step, 1, lane, k_ref, v_ref, m_rev)):
            acc_ref[n] += pv
            c_ref[n] = cn
            new.append(jnp.max(cn))
        return (*starts, *new)

    lax.while_loop(cond, body, (*[ks for _, ks, _, _ in jobs], *[jnp.max(c) for _, c in results]))
    for n in range(njobs):
        out = acc_ref[n, 0:SB_QROWS, :]
        for h in range(1, n_half):
            out = jnp.where(hmasks[h], acc_ref[n, h * SB_QROWS:(h + 1) * SB_QROWS, :], out)
        sl = slice(n * SB_QROWS, (n + 1) * SB_QROWS)
        o_ref[sl, :] = (out * _silu(gate[sl])).astype(o_ref.dtype)


def _sb_core(proj):
    s = proj.shape[0]
    nb = D_MODEL // LANES
    tq = SB_TQ
    return pl.pallas_call(
        _sb_kernel,
        out_shape=jax.ShapeDtypeStruct((s, D_MODEL), BF16),
        grid=(nb, s // tq),
        in_specs=[pl.BlockSpec((tq, LANES), lambda p, i: (i, p)),
                  pl.BlockSpec((s, LANES), lambda p, i: (0, nb + p)),
                  pl.BlockSpec((s, LANES), lambda p, i: (0, 2 * nb + p)),
                  pl.BlockSpec((tq, LANES), lambda p, i: (i, 3 * nb + p))],
        out_specs=pl.BlockSpec((tq, LANES), lambda p, i: (i, p)),
        scratch_shapes=[pltpu.VMEM((tq // SB_QROWS, LANES // SB_HEAD_DIM * SB_QROWS, LANES), F32)] * 2,
        compiler_params=_params("arbitrary", "arbitrary"),
        name="sb_attention",
    )(proj, proj, proj, proj)


def _hgrn_kernel(q_ref, f_ref, i_ref, g_ref, lb_ref, ng_ref, o_ref, st_ref, gc_ref, kc_ref, dg_ref, *, chunk):
    @pl.when(pl.program_id(1) == 0)
    def _():
        st_ref[...] = jnp.zeros_like(st_ref)

    c = chunk
    sub = HG_SUB
    nsub = c // sub
    row = lax.broadcasted_iota(jnp.int32, (c, c), 0)
    col = lax.broadcasted_iota(jnp.int32, (c, c), 1)
    tri = jnp.where(row >= col, 1.0, 0.0).astype(BF16)
    krow = lax.broadcasted_iota(jnp.int32, (c, LANES), 0)
    lb = lb_ref[...]
    nchunk = q_ref.shape[0] // c
    chunks = [slice(ci * c, (ci + 1) * c) for ci in range(nchunk)]

    q = _silu(q_ref[...].astype(F32))
    f = lb + (1.0 - lb) * _sigmoid(f_ref[...].astype(F32))
    k = 1.0 - f
    vb = i_ref[...].astype(BF16)
    logf = jnp.log2(f)
    g = jnp.concatenate([_dot_exact_left(tri, logf[ch]) for ch in chunks], axis=0)
    gc_ref[...] = g
    kc_ref[...] = k
    dg_ref[...] = jnp.zeros_like(dg_ref)
    crosses = []
    for ch in chunks:
        per_sub = [jnp.zeros((sub, c), F32)]
        for si in range(1, nsub):
            lo = ch.start + si * sub
            ref = gc_ref[lo - 1:lo, :]
            kin = jnp.where(krow < si * sub, k[ch] * jnp.exp2(ref - g[ch]), 0.0)
            qin = q[lo:lo + sub, :] * jnp.exp2(g[lo:lo + sub, :] - ref)
            per_sub.append(_dot_nt(qin.astype(BF16), kin.astype(BF16)))
        crosses.append(per_sub)
    o_intra = []
    for ch, per_sub in zip(chunks, crosses):
        for si in range(nsub):
            for rb in range(sub // SUBLANES):
                r0 = ch.start + si * sub + rb * SUBLANES
                gb, qb = g[r0:r0 + SUBLANES, :], q[r0:r0 + SUBLANES, :]
                for s in range(si * sub, si * sub + (rb + 1) * SUBLANES):
                    e = jnp.exp2(gb - gc_ref[ch.start + s:ch.start + s + 1, :])
                    dg_ref[r0:r0 + SUBLANES, s:s + 1] = jnp.sum(
                        qb * e * kc_ref[ch.start + s:ch.start + s + 1, :], axis=1, keepdims=True)
        sc = jnp.where(row >= col, jnp.where(row // sub == col // sub, dg_ref[ch, :],
                                             jnp.concatenate(per_sub, axis=0)), 0.0)
        o_intra.append(_dot(sc.astype(BF16), vb[ch]))
    qd = (q * jnp.exp2(g)).astype(BF16)
    updates, decays = [], []
    for ch in chunks:
        glast = gc_ref[ch.stop - 1:ch.stop, :]
        updates.append(_dot_tn(vb[ch], (k[ch] * jnp.exp2(glast - g[ch])).astype(BF16)))
        decays.append(jnp.exp2(glast))
    st = st_ref[...]
    outs = []
    for ci, ch in enumerate(chunks):
        outs.append(o_intra[ci] + _dot_nt(qd[ch], st.astype(BF16)))
        st = st * decays[ci] + updates[ci]
    st_ref[...] = st
    o = jnp.concatenate(outs, axis=0)
    ms = jnp.mean(o * o, axis=-1, keepdims=True)
    o_ref[...] = (o * lax.rsqrt(ms + EPS) * ng_ref[...] * _silu(g_ref[...].astype(F32))).astype(o_ref.dtype)


def _hgrn_core(proj, lb, norm_g, rows=512, chunk=HG_CHUNK):
    s = proj.shape[0]
    rows = min(rows, s)
    nh = HG_HEADS
    blk = lambda off: pl.BlockSpec((rows, LANES), lambda h, t: (t, off * nh + h))
    vec = pl.BlockSpec((1, LANES), lambda h, t: (0, h))
    return pl.pallas_call(
        functools.partial(_hgrn_kernel, chunk=chunk),
        out_shape=jax.ShapeDtypeStruct((s, D_MODEL), BF16),
        grid=(nh, s // rows),
        in_specs=[blk(0), blk(1), blk(2), blk(3), vec, vec],
        out_specs=pl.BlockSpec((rows, LANES), lambda h, t: (t, h)),
        scratch_shapes=[pltpu.VMEM((HG_HEAD_DIM, HG_HEAD_DIM), F32),
                        pltpu.VMEM((rows, LANES), F32), pltpu.VMEM((rows, LANES), F32),
                        pltpu.VMEM((rows, chunk), F32)],
        compiler_params=_params("arbitrary", "arbitrary"),
        name="hgrn2",
    )(proj, proj, proj, proj, lb.reshape(1, -1), norm_g.reshape(1, -1))


def _m2_kernel(z_ref, xs_ref, bc_ref, dt_ref, cw_ref, cb_ref, dtb_ref, alog_ref, dx_ref, ng_ref, ex_ref,
               o_ref, xbuf_ref, act_ref, st_ref, cumt_ref, dtx_ref, cumx_ref, y_ref):
    L = M2_CHUNK
    w = M2_WIDTH
    slab = 512

    @pl.when(pl.program_id(0) == 0)
    def _():
        xbuf_ref[0:SUBLANES, :] = jnp.zeros((SUBLANES, 2 * w), F32)
        st_ref[...] = jnp.zeros_like(st_ref)

    xbuf_ref[SUBLANES:SUBLANES + L, 0:w] = xs_ref[...].astype(F32)
    xbuf_ref[SUBLANES:SUBLANES + L, w:2 * w] = bc_ref[...].astype(F32)
    sub = lax.broadcasted_iota(jnp.int32, (L // SUBLANES, SUBLANES, slab), 1)
    for c0 in range(0, 2 * w, slab):
        cols = slice(c0, c0 + slab)
        window = xbuf_ref[:, cols].reshape(L // SUBLANES + 1, SUBLANES, slab)
        acc = cb_ref[:, cols] + cw_ref[M2_CONV - 1:M2_CONV, cols] * xbuf_ref[SUBLANES:, cols]
        for back in range(1, M2_CONV):
            rot = pltpu.roll(window, back, axis=1)
            shifted = jnp.where(sub < back, rot[:-1], rot[1:]).reshape(L, slab)
            acc = acc + cw_ref[M2_CONV - 1 - back:M2_CONV - back, cols] * shifted
        act_ref[:, cols] = _silu(acc)
    xbuf_ref[0:SUBLANES, :] = xbuf_ref[L:L + SUBLANES, :]

    row = lax.broadcasted_iota(jnp.int32, (L, L), 0)
    col = lax.broadcasted_iota(jnp.int32, (L, L), 1)
    tril = row >= col
    tri = jnp.where(tril, 1.0, 0.0).astype(BF16)
    dt = _softplus(dt_ref[...] + dtb_ref[...])
    adt = dt * (-jnp.exp(alog_ref[...]))
    cum = _dot_exact_left(tri, adt)
    cumt_ref[...] = cum.T
    ex = ex_ref[...]
    dtx_ref[...] = _dot_exact_right(dt, ex)
    cumx_ref[...] = _dot_exact_right(cum, ex)
    lane_half = lax.broadcasted_iota(jnp.int32, (L, LANES), 1) // M2_HEAD_DIM

    for g in range(M2_GROUPS):
        bg = act_ref[:, w + g * M2_STATE: w + (g + 1) * M2_STATE].astype(BF16)
        cg = act_ref[:, w + (M2_GROUPS + g) * M2_STATE: w + (M2_GROUPS + g + 1) * M2_STATE].astype(BF16)
        cbm = _dot_nt(cg, bg)
        heads_per_group = M2_HEADS // M2_GROUPS
        for pp in range(heads_per_group // 2):
            pair = g * (heads_per_group // 2) + pp
            sl = slice(pair * LANES, (pair + 1) * LANES)
            xs_p = act_ref[:, sl]
            xdt = xs_p * dtx_ref[:, sl]
            xdt_b = xdt.astype(BF16)
            cumx_p = cumx_ref[:, sl]
            cum_last = cumx_ref[L - 1:L, sl]
            yd = jnp.zeros((L, LANES), F32)
            for hh in range(2):
                h = 2 * pair + hh
                cc = cumx_ref[:, h * M2_HEAD_DIM:h * M2_HEAD_DIM + 1]
                cr = cumt_ref[h:h + 1, :]
                dec = jnp.exp(jnp.where(tril, cc - cr, -jnp.inf))
                yh = _dot((cbm * dec).astype(BF16), xdt_b)
                yd = jnp.where(lane_half == hh, yh, yd)
            st = st_ref[pair]
            yoff = _dot(cg, st.astype(BF16)) * jnp.exp(cumx_p)
            dte = jnp.exp(cum_last - cumx_p)
            st_ref[pair] = st * jnp.exp(cum_last) + _dot_tn(bg, (xdt * dte).astype(BF16))
            y = yd + yoff + dx_ref[:, sl] * xs_p
            y_ref[:, sl] = y * _silu(z_ref[:, sl].astype(F32))
        gs = w // M2_GROUPS
        yg = y_ref[:, g * gs:(g + 1) * gs]
        ms = jnp.mean(yg * yg, axis=-1, keepdims=True)
        o_ref[:, g * gs:(g + 1) * gs] = (yg * lax.rsqrt(ms + EPS) * ng_ref[:, g * gs:(g + 1) * gs]).astype(o_ref.dtype)


def _m2_core(proj, dt_raw, conv_w, conv_b, dt_bias, a_log, d_skip, norm_g):
    s = proj.shape[0]
    L, w = M2_CHUNK, M2_WIDTH
    pad = LANES - M2_HEADS
    dtb = jnp.pad(dt_bias, (0, pad)).reshape(1, LANES)
    alog = jnp.pad(a_log, (0, pad)).reshape(1, LANES)
    dx = jnp.repeat(d_skip, M2_HEAD_DIM).reshape(1, w)
    ex = (jnp.arange(w)[None, :] // M2_HEAD_DIM == jnp.arange(LANES)[:, None]).astype(BF16)
    full = lambda shape: pl.BlockSpec(shape, lambda c: (0,) * len(shape))
    return pl.pallas_call(
        _m2_kernel,
        out_shape=jax.ShapeDtypeStruct((s, w), BF16),
        grid=(s // L,),
        in_specs=[pl.BlockSpec((L, w), lambda c: (c, 0)),
                  pl.BlockSpec((L, w), lambda c: (c, 1)),
                  pl.BlockSpec((L, w), lambda c: (c, 2)),
                  pl.BlockSpec((L, LANES), lambda c: (c, 0)),
                  full((M2_CONV, 2 * w)), full((1, 2 * w)), full((1, LANES)), full((1, LANES)),
                  full((1, w)), full((1, w)), full((LANES, w))],
        out_specs=pl.BlockSpec((L, w), lambda c: (c, 0)),
        scratch_shapes=[pltpu.VMEM((L + SUBLANES, 2 * w), F32),
                        pltpu.VMEM((L, 2 * w), F32),
                        pltpu.VMEM((M2_HEADS // 2, M2_STATE, LANES), F32),
                        pltpu.VMEM((LANES, L), F32),
                        pltpu.VMEM((L, w), F32), pltpu.VMEM((L, w), F32), pltpu.VMEM((L, w), F32)],
        compiler_params=_params("arbitrary"),
        name="mamba2_ssd",
    )(proj, proj, proj, dt_raw, conv_w, conv_b.reshape(1, -1), dtb, alog, dx, norm_g.reshape(1, -1), ex)


def _ret_kernel(q_ref, k_ref, v_ref, g_ref, ang_ref, lg_ref, o_ref, st_ref, intra_ref, qdec_ref, kdec_ref,
                cos_ref, sin_ref):
    c = pl.program_id(0)
    L = RET_CHUNK
    half = RET_QK_DIM // 2
    heads = range(RET_HEADS)

    @pl.when(c == 0)
    def _():
        st_ref[...] = jnp.zeros_like(st_ref)
        row = lax.broadcasted_iota(jnp.int32, (L, L), 0)
        col = lax.broadcasted_iota(jnp.int32, (L, L), 1)
        rel = (row - col).astype(F32)
        rowf = row.astype(F32)
        for h in heads:
            lg = lg_ref[h]
            intra_ref[h] = jnp.where(rel >= 0, jnp.exp(lg * jnp.maximum(rel, 0.0)), 0.0)
            qdec_ref[h] = jnp.exp(lg * (rowf + 1.0))
            kdec_ref[h] = jnp.exp(lg * (L - 1.0 - rowf))
        ph = lax.broadcasted_iota(jnp.int32, (L, half), 0).astype(F32) * ang_ref[...]
        cos_ref[...] = jnp.cos(ph)
        sin_ref[...] = jnp.sin(ph)

    @pl.when(c > 0)
    def _():
        step = float(L) * ang_ref[...]
        cd, sd = jnp.cos(step), jnp.sin(step)
        c0, s0 = cos_ref[...], sin_ref[...]
        cos_ref[...] = c0 * cd - s0 * sd
        sin_ref[...] = s0 * cd + c0 * sd

    cos, sin = cos_ref[...], sin_ref[...]

    def rot(x):
        x0, x1 = x[:, :half], x[:, half:]
        return jnp.concatenate([x0 * cos - x1 * sin, x1 * cos + x0 * sin], axis=1)

    qk = lambda ref, h: rot(ref[:, h * RET_QK_DIM:(h + 1) * RET_QK_DIM].astype(F32))
    vcols = [slice(h * RET_V_DIM, (h + 1) * RET_V_DIM) for h in heads]
    qbs = [qk(q_ref, h).astype(BF16) for h in heads]
    ks = [qk(k_ref, h) * (RET_QK_DIM ** -0.5) for h in heads]
    vbs = [v_ref[:, vcols[h]].astype(BF16) for h in heads]
    scores = [(_dot_nt(qbs[h], ks[h].astype(BF16)) * intra_ref[h]).astype(BF16) for h in heads]
    inters = [_dot(qbs[h], st_ref[h].astype(BF16)) for h in heads]
    for h in heads:
        kd = ks[h] * jnp.concatenate([kdec_ref[h]] * (RET_QK_DIM // LANES), axis=1)
        st_ref[h] = st_ref[h] * jnp.exp(lg_ref[h][:, 0:1] * float(L)) + _dot_tn(kd.astype(BF16), vbs[h])
    for h in heads:
        o = _dot(scores[h], vbs[h]) + jnp.concatenate([qdec_ref[h]] * (RET_V_DIM // LANES), axis=1) * inters[h]
        ms = jnp.mean(o * o, axis=-1, keepdims=True)
        o_ref[:, vcols[h]] = (o * lax.rsqrt(ms + EPS) * _silu(g_ref[:, vcols[h]].astype(F32))).astype(o_ref.dtype)


def _ret_core(proj):
    s = proj.shape[0]
    L = RET_CHUNK
    nh = RET_HEADS
    half = RET_QK_DIM // 2
    angle = (1.0 / (RET_ROT_BASE ** jnp.linspace(0.0, 1.0, half, dtype=F32))).reshape(1, half)
    log_g = jnp.log1p(-jnp.exp2(-5.0 - jnp.arange(nh, dtype=F32)))
    lg = jnp.broadcast_to(log_g[:, None, None], (nh, 1, LANES))
    return pl.pallas_call(
        _ret_kernel,
        out_shape=jax.ShapeDtypeStruct((s, RET_V_WIDTH), BF16),
        grid=(s // L,),
        in_specs=[pl.BlockSpec((L, D_MODEL), lambda c: (c, 0)),
                  pl.BlockSpec((L, D_MODEL), lambda c: (c, 1)),
                  pl.BlockSpec((L, RET_V_WIDTH), lambda c: (c, 1)),
                  pl.BlockSpec((L, RET_V_WIDTH), lambda c: (c, 2)),
                  pl.BlockSpec((1, half), lambda c: (0, 0)),
                  pl.BlockSpec((nh, 1, LANES), lambda c: (0, 0, 0))],
        out_specs=pl.BlockSpec((L, RET_V_WIDTH), lambda c: (c, 0)),
        scratch_shapes=[pltpu.VMEM((nh, RET_QK_DIM, RET_V_DIM), F32),
                        pltpu.VMEM((nh, L, L), F32), pltpu.VMEM((nh, L, LANES), F32),
                        pltpu.VMEM((nh, L, LANES), F32),
                        pltpu.VMEM((L, half), F32), pltpu.VMEM((L, half), F32)],
        compiler_params=_params("arbitrary"),
        name="retention",
    )(proj, proj, proj, proj, angle, lg)


def _ret_permute_kernel(w_ref, o_ref):
    src = lax.broadcasted_iota(jnp.int32, (RET_QK_DIM, RET_QK_DIM), 0)
    dst = lax.broadcasted_iota(jnp.int32, (RET_QK_DIM, RET_QK_DIM), 1)
    half = RET_QK_DIM // 2
    perm = jnp.where(src == 2 * (dst % half) + dst // half, 1.0, 0.0).astype(BF16)
    o_ref[...] = _dot(w_ref[...], perm).astype(o_ref.dtype)


def _ret_permute_qk(w):
    d, n = w.shape
    blk = pl.BlockSpec((d, RET_QK_DIM), lambda hb: (0, hb))
    return pl.pallas_call(
        _ret_permute_kernel,
        out_shape=jax.ShapeDtypeStruct((d, n), w.dtype),
        grid=(2 * RET_HEADS,),
        in_specs=[blk],
        out_specs=blk,
        input_output_aliases={0: 0},
        compiler_params=_params("arbitrary"),
        name="ret_permute_qk",
    )(w)


def kernel(x, norm_g, sb_w_in, sb_w_out, hg_w_in, hg_lb_logits, hg_norm_g, hg_w_out, m2_w_in, m2_conv_w,
           m2_conv_b, m2_dt_bias, m2_a_log, m2_d, m2_norm_g, m2_w_out, ret_w_in, ret_w_out, final_g):
    b, s, d = x.shape
    depth = norm_g.shape[0]
    lb_cum = jnp.cumsum(jax.nn.softmax(hg_lb_logits.astype(F32), axis=0), axis=0)
    lower_bounds = lb_cum - lb_cum[0]
    outs = []
    for bi in range(b):
        xb = x[bi]
        for i in range(depth):
            m, j = i % 4, i // 4
            fg = final_g if i == depth - 1 else None
            if m == 0:
                proj = _norm_proj(xb, norm_g[i], sb_w_in[j].astype(BF16))
                xb = _out_proj(_sb_core(proj), sb_w_out[j], xb, fg)
            elif m == 1:
                proj = _norm_proj(xb, norm_g[i], hg_w_in[j].astype(BF16))
                og = _hgrn_core(proj, lower_bounds[i], hg_norm_g[j])
                xb = _out_proj(og, hg_w_out[j], xb, fg)
            elif m == 2:
                n_main = M2_WIDTH + M2_WIDTH + 2 * M2_GROUPS * M2_STATE
                w_all = m2_w_in[j].astype(BF16)
                w_dt = jnp.pad(w_all[:, n_main:], ((0, 0), (0, LANES - M2_HEADS)))
                proj, dt_raw = _m2_in_proj(xb, norm_g[i], w_all, w_dt, n_main)
                og = _m2_core(proj, dt_raw, m2_conv_w[j], m2_conv_b[j], m2_dt_bias[j], m2_a_log[j], m2_d[j],
                              m2_norm_g[j])
                xb = _out_proj(og, m2_w_out[j], xb, fg)
            else:
                w_in = _ret_permute_qk(ret_w_in[j].astype(BF16))
                proj = _norm_proj(xb, norm_g[i], w_in)
                xb = _out_proj(_ret_core(proj), ret_w_out[j], xb, fg)
        outs.append(xb)
    return outs[0][None] if b == 1 else jnp.stack(outs, axis=0)
```

```python
import functools
import math

import jax
import jax.numpy as jnp
from jax import lax
from jax.experimental import pallas as pl
from jax.experimental.pallas import tpu as pltpu

F32 = jnp.float32
BF16 = jnp.bfloat16
EPS = 1e-6
LANES = 128
SUBLANES = 8
MXU_COLS = 256
VMEM_LIMIT = 48 * 1024 * 1024

D_MODEL = 1024
SB_HEADS, SB_HEAD_DIM, SB_BLOCK = 16, 64, 128
HG_HEADS, HG_HEAD_DIM, HG_CHUNK, HG_SUB = 8, 128, 64, 16
M2_WIDTH, M2_HEADS, M2_HEAD_DIM, M2_GROUPS, M2_STATE, M2_CONV, M2_CHUNK = 2048, 32, 64, 8, 128, 4, 128
RET_HEADS, RET_QK_DIM, RET_V_DIM, RET_V_WIDTH, RET_CHUNK = 4, 256, 512, 2048, 128
RET_ROT_BASE = 10000.0
SB_UNDERFLOW = -104.0
SB_QROWS = 64
SB_WINDOW = 2
SB_TQ = 1024


def _dot(a, b):
    return jnp.dot(a, b, preferred_element_type=F32)


def _dot_nt(a, b):
    return lax.dot_general(a, b, (((1,), (1,)), ((), ())), preferred_element_type=F32)


def _dot_tn(a, b):
    return lax.dot_general(a, b, (((0,), (0,)), ((), ())), preferred_element_type=F32)


def _split3(x):
    hi = x.astype(BF16)
    r1 = x - hi.astype(F32)
    mid = r1.astype(BF16)
    lo = (r1 - mid.astype(F32)).astype(BF16)
    return hi, mid, lo


def _dot_exact_right(x, m01):
    hi, mid, lo = _split3(x)
    return _dot(hi, m01) + _dot(mid, m01) + _dot(lo, m01)


def _dot_exact_left(m01, x):
    hi, mid, lo = _split3(x)
    return _dot(m01, hi) + _dot(m01, mid) + _dot(m01, lo)


def _sigmoid(x):
    return 1.0 / (1.0 + jnp.exp(-x))


def _silu(x):
    return x * _sigmoid(x)


def _softplus(x):
    u = jnp.exp(-jnp.abs(x))
    w = 1.0 + u
    log1p_u = jnp.where(w == 1.0, u, jnp.log(w) * (u / (w - 1.0)))
    return jnp.maximum(x, 0.0) + log1p_u


def _params(*sem):
    return pltpu.CompilerParams(dimension_semantics=sem, vmem_limit_bytes=VMEM_LIMIT)


def _rms_to_bf16(x_ref, g_ref, h_ref):
    xf = x_ref[...]
    ms = jnp.mean(xf * xf, axis=-1, keepdims=True)
    h_ref[...] = (xf * lax.rsqrt(ms + EPS) * g_ref[...]).astype(BF16)


def _norm_proj_kernel(x_ref, g_ref, w_ref, o_ref, h_ref):
    @pl.when(pl.program_id(1) == 0)
    def _():
        _rms_to_bf16(x_ref, g_ref, h_ref)

    o_ref[...] = _dot(h_ref[...], w_ref[...].astype(BF16)).astype(o_ref.dtype)


def _norm_proj(x, g, w, tn=2048, tm=1024):
    s, d = x.shape
    n = w.shape[1]
    tm = min(tm, s)
    return pl.pallas_call(
        _norm_proj_kernel,
        out_shape=jax.ShapeDtypeStruct((s, n), BF16),
        grid=(s // tm, n // tn),
        in_specs=[pl.BlockSpec((tm, d), lambda i, j: (i, 0)),
                  pl.BlockSpec((1, d), lambda i, j: (0, 0)),
                  pl.BlockSpec((d, tn), lambda i, j: (0, j))],
        out_specs=pl.BlockSpec((tm, tn), lambda i, j: (i, j)),
        scratch_shapes=[pltpu.VMEM((tm, d), BF16)],
        compiler_params=_params("arbitrary", "arbitrary"),
        name="norm_proj",
    )(x, g.reshape(1, d), w)


def _m2_proj_kernel(x_ref, g_ref, w_ref, dtw_ref, o_ref, dt_ref, h_ref):
    @pl.when(pl.program_id(1) == 0)
    def _():
        _rms_to_bf16(x_ref, g_ref, h_ref)
        dt_ref[...] = _dot(h_ref[...], dtw_ref[...].astype(BF16))

    o_ref[...] = _dot(h_ref[...], w_ref[...].astype(BF16)).astype(o_ref.dtype)


def _m2_in_proj(x, g, w, w_dt, n, tn=2048, tm=1024):
    s, d = x.shape
    tm = min(tm, s)
    return pl.pallas_call(
        _m2_proj_kernel,
        out_shape=(jax.ShapeDtypeStruct((s, n), BF16), jax.ShapeDtypeStruct((s, LANES), F32)),
        grid=(s // tm, n // tn),
        in_specs=[pl.BlockSpec((tm, d), lambda i, j: (i, 0)),
                  pl.BlockSpec((1, d), lambda i, j: (0, 0)),
                  pl.BlockSpec((d, tn), lambda i, j: (0, j)),
                  pl.BlockSpec((d, LANES), lambda i, j: (0, 0))],
        out_specs=(pl.BlockSpec((tm, tn), lambda i, j: (i, j)),
                   pl.BlockSpec((tm, LANES), lambda i, j: (i, 0))),
        scratch_shapes=[pltpu.VMEM((tm, d), BF16)],
        compiler_params=_params("arbitrary", "arbitrary"),
        name="m2_in_proj",
    )(x, g.reshape(1, d), w, w_dt)


def _out_proj_kernel(og_ref, w_ref, x_ref, *rest, final):
    if final:
        fg_ref, o_ref, wb_ref = rest
    else:
        o_ref, wb_ref = rest

    @pl.when(pl.program_id(0) == 0)
    def _():
        wb_ref[...] = w_ref[...].astype(BF16)

    y = x_ref[...] + _dot(og_ref[...], wb_ref[...])
    if final:
        ms = jnp.mean(y * y, axis=-1, keepdims=True)
        y = y * lax.rsqrt(ms + EPS) * fg_ref[...]
    o_ref[...] = y


def _out_proj(og, w, x, final_g=None, tm=1024):
    s, width = og.shape
    tm = min(tm, s)
    d = x.shape[1]
    in_specs = [pl.BlockSpec((tm, width), lambda i: (i, 0)),
                pl.BlockSpec((width, d), lambda i: (0, 0), pipeline_mode=pl.Buffered(1)),
                pl.BlockSpec((tm, d), lambda i: (i, 0))]
    args = [og, w, x]
    if final_g is not None:
        in_specs.append(pl.BlockSpec((1, d), lambda i: (0, 0)))
        args.append(final_g.reshape(1, d))
    return pl.pallas_call(
        functools.partial(_out_proj_kernel, final=final_g is not None),
        out_shape=jax.ShapeDtypeStruct((s, d), F32),
        grid=(s // tm,),
        in_specs=in_specs,
        out_specs=pl.BlockSpec((tm, d), lambda i: (i, 0)),
        scratch_shapes=[pltpu.VMEM((width, d), BF16)],
        compiler_params=_params("arbitrary"),
        name="out_proj",
    )(*args)


def _sb_spans(jobs, nblk, rel, k_ref, v_ref, m_rev):
    rows = jobs[0][0].shape[0]
    width = nblk * SB_BLOCK
    blk = lambda x, b: x[:, b * SB_BLOCK:(b + 1) * SB_BLOCK]
    starts = [pl.multiple_of(ks, SB_QROWS) for _, ks, _, _ in jobs]
    nzs = [_dot_nt(qs, k_ref[pl.ds(ks, width), :]) for (qs, _, _, _), ks in zip(jobs, starts)]
    masks, l1ms = [], []
    for (_, _, bound, _), nz in zip(jobs, nzs):
        mask = [rel < (bound - b * SB_BLOCK) for b in range(nblk)]
        l1m = [jnp.where(mask[b], jnp.minimum(blk(nz, b), 0.0) - jnp.log(1.0 + jnp.exp(-jnp.abs(blk(nz, b)))), 0.0)
               for b in range(nblk)]
        masks.append(mask)
        l1ms.append(l1m)
    revs = []
    for l1m in l1ms:
        stacked = jnp.concatenate(l1m, axis=0)
        hi = stacked.astype(BF16)
        lo = (stacked - hi.astype(F32)).astype(BF16)
        revs.append(_dot(jnp.concatenate([hi, lo], axis=1), m_rev))
    probs, cs = [], []
    for n, ((_, _, _, c), nz) in enumerate(zip(jobs, nzs)):
        parts = [None] * nblk
        for b in reversed(range(nblk)):
            e = jnp.exp(revs[n][b * rows:(b + 1) * rows] + c - blk(nz, b))
            parts[b] = jnp.where(masks[n][b], e, 0.0).astype(BF16)
            c = c + jnp.sum(l1ms[n][b], axis=1, keepdims=True)
        probs.append(jnp.concatenate(parts, axis=1))
        cs.append(c)
    return [(_dot(a, v_ref[pl.ds(ks, width), :]), c) for a, ks, c in zip(probs, starts, cs)]


def _sb_kernel(q_ref, k_ref, v_ref, g_ref, o_ref, acc_ref, c_ref):
    i = pl.program_id(1)
    njobs = q_ref.shape[0] // SB_QROWS
    nkeys = k_ref.shape[0]
    n_half = LANES // SB_HEAD_DIM
    rows = n_half * SB_QROWS
    span = SB_WINDOW * SB_BLOCK
    row = lax.broadcasted_iota(jnp.int32, (2 * SB_BLOCK, SB_BLOCK), 0) & (SB_BLOCK - 1)
    col = lax.broadcasted_iota(jnp.int32, (2 * SB_BLOCK, SB_BLOCK), 1)
    m_rev = jnp.where(row >= col, 1.0, 0.0).astype(BF16)
    lane_q = lax.broadcasted_iota(jnp.int32, (SB_QROWS, LANES), 1)
    hmasks = [(lane_q >= h * SB_HEAD_DIM) & (lane_q < (h + 1) * SB_HEAD_DIM) for h in range(n_half)]
    lane = lax.broadcasted_iota(jnp.int32, (rows, SB_BLOCK), 1)
    qoff = lax.broadcasted_iota(jnp.int32, (rows, SB_BLOCK), 0) & (SB_QROWS - 1)
    jobs = []
    for n in range(njobs):
        q = q_ref[n * SB_QROWS:(n + 1) * SB_QROWS, :] * (-(SB_HEAD_DIM ** -0.5))
        qs = jnp.concatenate([jnp.where(m, q, jnp.zeros_like(q)) for m in hmasks], axis=0)
        q0 = (i * njobs + n) * SB_QROWS
        ks = jnp.clip(q0 + SB_QROWS - span, 0, nkeys - span)
        jobs.append((qs, ks, q0 - ks, jnp.zeros((rows, 1), F32)))
    results = _sb_spans(jobs, SB_WINDOW, lane - qoff, k_ref, v_ref, m_rev)
    gate = g_ref[...].astype(F32)
    for n, (pv, c) in enumerate(results):
        acc_ref[n] = pv
        c_ref[n] = jnp.broadcast_to(c, c_ref.shape[1:])

    def cond(st):
        ends, cmaxes = st[:njobs], st[njobs:]
        return functools.reduce(jnp.logical_or, [jnp.logical_and(e > 0, cm > SB_UNDERFLOW)
                                                 for e, cm in zip(ends, cmaxes)])

    def body(st):
        ends = st[:njobs]
        starts = [jnp.maximum(e - SB_BLOCK, 0) for e in ends]
        step = [(qs, s0, e - s0, c_ref[n]) for n, ((qs, _, _, _), s0, e) in enumerate(zip(jobs, starts, ends))]
        new = []
        for n, (pv, cn) in enumerate(_sb_spans(step, 1, lane, k_ref, v_ref, m_rev)):
            acc_ref[n] += pv
            c_ref[n] = cn
            new.append(jnp.max(cn))
        return (*starts, *new)

    lax.while_loop(cond, body, (*[ks for _, ks, _, _ in jobs], *[jnp.max(c) for _, c in results]))
    for n in range(njobs):
        out = acc_ref[n, 0:SB_QROWS, :]
        for h in range(1, n_half):
            out = jnp.where(hmasks[h], acc_ref[n, h * SB_QROWS:(h + 1) * SB_QROWS, :], out)
        sl = slice(n * SB_QROWS, (n + 1) * SB_QROWS)
        o_ref[sl, :] = (out * _silu(gate[sl])).astype(o_ref.dtype)


def _sb_core(proj):
    s = proj.shape[0]
    nb = D_MODEL // LANES
    tq = SB_TQ
    return pl.pallas_call(
        _sb_kernel,
        out_shape=jax.ShapeDtypeStruct((s, D_MODEL), BF16),
        grid=(nb, s // tq),
        in_specs=[pl.BlockSpec((tq, LANES), lambda p, i: (i, p)),
                  pl.BlockSpec((s, LANES), lambda p, i: (0, nb + p)),
                  pl.BlockSpec((s, LANES), lambda p, i: (0, 2 * nb + p)),
                  pl.BlockSpec((tq, LANES), lambda p, i: (i, 3 * nb + p))],
        out_specs=pl.BlockSpec((tq, LANES), lambda p, i: (i, p)),
        scratch_shapes=[pltpu.VMEM((tq // SB_QROWS, LANES // SB_HEAD_DIM * SB_QROWS, LANES), F32)] * 2,
        compiler_params=_params("arbitrary", "arbitrary"),
        name="sb_attention",
    )(proj, proj, proj, proj)


def _hgrn_kernel(q_ref, f_ref, i_ref, g_ref, lb_ref, ng_ref, o_ref, st_ref, gc_ref, kc_ref, dg_ref, *, chunk):
    @pl.when(pl.program_id(1) == 0)
    def _():
        st_ref[...] = jnp.zeros_like(st_ref)

    c = chunk
    sub = HG_SUB
    nsub = c // sub
    row = lax.broadcasted_iota(jnp.int32, (c, c), 0)
    col = lax.broadcasted_iota(jnp.int32, (c, c), 1)
    tri = jnp.where(row >= col, 1.0, 0.0).astype(BF16)
    krow = lax.broadcasted_iota(jnp.int32, (c, LANES), 0)
    lb = lb_ref[...]
    nchunk = q_ref.shape[0] // c
    chunks = [slice(ci * c, (ci + 1) * c) for ci in range(nchunk)]

    q = _silu(q_ref[...].astype(F32))
    f = lb + (1.0 - lb) * _sigmoid(f_ref[...].astype(F32))
    k = 1.0 - f
    vb = i_ref[...].astype(BF16)
    logf = jnp.log2(f)
    g = jnp.concatenate([_dot_exact_left(tri, logf[ch]) for ch in chunks], axis=0)
    gc_ref[...] = g
    kc_ref[...] = k
    dg_ref[...] = jnp.zeros_like(dg_ref)
    crosses = []
    for ch in chunks:
        per_sub = [jnp.zeros((sub, c), F32)]
        for si in range(1, nsub):
            lo = ch.start + si * sub
            ref = gc_ref[lo - 1:lo, :]
            kin = jnp.where(krow < si * sub, k[ch] * jnp.exp2(ref - g[ch]), 0.0)
            qin = q[lo:lo + sub, :] * jnp.exp2(g[lo:lo + sub, :] - ref)
            per_sub.append(_dot_nt(qin.astype(BF16), kin.astype(BF16)))
        crosses.append(per_sub)
    o_intra = []
    for ch, per_sub in zip(chunks, crosses):
        for si in range(nsub):
            for rb in range(sub // SUBLANES):
                r0 = ch.start + si * sub + rb * SUBLANES
                gb, qb = g[r0:r0 + SUBLANES, :], q[r0:r0 + SUBLANES, :]
                for s in range(si * sub, si * sub + (rb + 1) * SUBLANES):
                    e = jnp.exp2(gb - gc_ref[ch.start + s:ch.start + s + 1, :])
                    dg_ref[r0:r0 + SUBLANES, s:s + 1] = jnp.sum(
                        qb * e * kc_ref[ch.start + s:ch.start + s + 1, :], axis=1, keepdims=True)
        sc = jnp.where(row >= col, jnp.where(row // sub == col // sub, dg_ref[ch, :],
                                             jnp.concatenate(per_sub, axis=0)), 0.0)
        o_intra.append(_dot(sc.astype(BF16), vb[ch]))
    qd = (q * jnp.exp2(g)).astype(BF16)
    updates, decays = [], []
    for ch in chunks:
        glast = gc_ref[ch.stop - 1:ch.stop, :]
        updates.append(_dot_tn(vb[ch], (k[ch] * jnp.exp2(glast - g[ch])).astype(BF16)))
        decays.append(jnp.exp2(glast))
    st = st_ref[...]
    outs = []
    for ci, ch in enumerate(chunks):
        outs.append(o_intra[ci] + _dot_nt(qd[ch], st.astype(BF16)))
        st = st * decays[ci] + updates[ci]
    st_ref[...] = st
    o = jnp.concatenate(outs, axis=0)
    ms = jnp.mean(o * o, axis=-1, keepdims=True)
    o_ref[...] = (o * lax.rsqrt(ms + EPS) * ng_ref[...] * _silu(g_ref[...].astype(F32))).astype(o_ref.dtype)


def _hgrn_core(proj, lb, norm_g, rows=2048, chunk=HG_CHUNK):
    s = proj.shape[0]
    rows = min(rows, s)
    nh = HG_HEADS
    blk = lambda off: pl.BlockSpec((rows, LANES), lambda h, t: (t, off * nh + h))
    vec = pl.BlockSpec((1, LANES), lambda h, t: (0, h))
    return pl.pallas_call(
        functools.partial(_hgrn_kernel, chunk=chunk),
        out_shape=jax.ShapeDtypeStruct((s, D_MODEL), BF16),
        grid=(nh, s // rows),
        in_specs=[blk(0), blk(1), blk(2), blk(3), vec, vec],
        out_specs=pl.BlockSpec((rows, LANES), lambda h, t: (t, h)),
        scratch_shapes=[pltpu.VMEM((HG_HEAD_DIM, HG_HEAD_DIM), F32),
                        pltpu.VMEM((rows, LANES), F32), pltpu.VMEM((rows, LANES), F32),
                        pltpu.VMEM((rows, chunk), F32)],
        compiler_params=_params("arbitrary", "arbitrary"),
        name="hgrn2",
    )(proj, proj, proj, proj, lb.reshape(1, -1), norm_g.reshape(1, -1))


def _m2_kernel(z_ref, xs_ref, bc_ref, dt_ref, cw_ref, cb_ref, dtb_ref, alog_ref, dx_ref, ng_ref, ex_ref,
               o_ref, xbuf_ref, act_ref, st_ref, cumt_ref, dtx_ref, cumx_ref, y_ref):
    L = M2_CHUNK
    w = M2_WIDTH
    slab = 512

    @pl.when(pl.program_id(0) == 0)
    def _():
        xbuf_ref[0:SUBLANES, :] = jnp.zeros((SUBLANES, 2 * w), F32)
        st_ref[...] = jnp.zeros_like(st_ref)

    xbuf_ref[SUBLANES:SUBLANES + L, 0:w] = xs_ref[...].astype(F32)
    xbuf_ref[SUBLANES:SUBLANES + L, w:2 * w] = bc_ref[...].astype(F32)
    sub = lax.broadcasted_iota(jnp.int32, (L // SUBLANES, SUBLANES, slab), 1)
    for c0 in range(0, 2 * w, slab):
        cols = slice(c0, c0 + slab)
        window = xbuf_ref[:, cols].reshape(L // SUBLANES + 1, SUBLANES, slab)
        acc = cb_ref[:, cols] + cw_ref[M2_CONV - 1:M2_CONV, cols] * xbuf_ref[SUBLANES:, cols]
        for back in range(1, M2_CONV):
            rot = pltpu.roll(window, back, axis=1)
            shifted = jnp.where(sub < back, rot[:-1], rot[1:]).reshape(L, slab)
            acc = acc + cw_ref[M2_CONV - 1 - back:M2_CONV - back, cols] * shifted
        act_ref[:, cols] = _silu(acc)
    xbuf_ref[0:SUBLANES, :] = xbuf_ref[L:L + SUBLANES, :]

    row = lax.broadcasted_iota(jnp.int32, (L, L), 0)
    col = lax.broadcasted_iota(jnp.int32, (L, L), 1)
    tril = row >= col
    tri = jnp.where(tril, 1.0, 0.0).astype(BF16)
    dt = _softplus(dt_ref[...] + dtb_ref[...])
    adt = dt * (-jnp.exp(alog_ref[...]))
    cum = _dot_exact_left(tri, adt)
    cumt_ref[...] = cum.T
    ex = ex_ref[...]
    dtx_ref[...] = _dot_exact_right(dt, ex)
    cumx_ref[...] = _dot_exact_right(cum, ex)
    lane_half = lax.broadcasted_iota(jnp.int32, (L, LANES), 1) // M2_HEAD_DIM

    for g in range(M2_GROUPS):
        bg = act_ref[:, w + g * M2_STATE: w + (g + 1) * M2_STATE].astype(BF16)
        cg = act_ref[:, w + (M2_GROUPS + g) * M2_STATE: w + (M2_GROUPS + g + 1) * M2_STATE].astype(BF16)
        cbm = _dot_nt(cg, bg)
        heads_per_group = M2_HEADS // M2_GROUPS
        for pp in range(heads_per_group // 2):
            pair = g * (heads_per_group // 2) + pp
            sl = slice(pair * LANES, (pair + 1) * LANES)
            xs_p = act_ref[:, sl]
            xdt = xs_p * dtx_ref[:, sl]
            xdt_b = xdt.astype(BF16)
            cumx_p = cumx_ref[:, sl]
            cum_last = cumx_ref[L - 1:L, sl]
            yd = jnp.zeros((L, LANES), F32)
            for hh in range(2):
                h = 2 * pair + hh
                cc = cumx_ref[:, h * M2_HEAD_DIM:h * M2_HEAD_DIM + 1]
                cr = cumt_ref[h:h + 1, :]
                dec = jnp.exp(jnp.where(tril, cc - cr, -jnp.inf))
                yh = _dot((cbm * dec).astype(BF16), xdt_b)
                yd = jnp.where(lane_half == hh, yh, yd)
            st = st_ref[pair]
            yoff = _dot(cg, st.astype(BF16)) * jnp.exp(cumx_p)
            dte = jnp.exp(cum_last - cumx_p)
            st_ref[pair] = st * jnp.exp(cum_last) + _dot_tn(bg, (xdt * dte).astype(BF16))
            y = yd + yoff + dx_ref[:, sl] * xs_p
            y_ref[:, sl] = y * _silu(z_ref[:, sl].astype(F32))
        gs = w // M2_GROUPS
        yg = y_ref[:, g * gs:(g + 1) * gs]
        ms = jnp.mean(yg * yg, axis=-1, keepdims=True)
        o_ref[:, g * gs:(g + 1) * gs] = (yg * lax.rsqrt(ms + EPS) * ng_ref[:, g * gs:(g + 1) * gs]).astype(o_ref.dtype)


def _m2_core(proj, dt_raw, conv_w, conv_b, dt_bias, a_log, d_skip, norm_g):
    s = proj.shape[0]
    L, w = M2_CHUNK, M2_WIDTH
    pad = LANES - M2_HEADS
    dtb = jnp.pad(dt_bias, (0, pad)).reshape(1, LANES)
    alog = jnp.pad(a_log, (0, pad)).reshape(1, LANES)
    dx = jnp.repeat(d_skip, M2_HEAD_DIM).reshape(1, w)
    ex = (jnp.arange(w)[None, :] // M2_HEAD_DIM == jnp.arange(LANES)[:, None]).astype(BF16)
    full = lambda shape: pl.BlockSpec(shape, lambda c: (0,) * len(shape))
    return pl.pallas_call(
        _m2_kernel,
        out_shape=jax.ShapeDtypeStruct((s, w), BF16),
        grid=(s // L,),
        in_specs=[pl.BlockSpec((L, w), lambda c: (c, 0)),
                  pl.BlockSpec((L, w), lambda c: (c, 1)),
                  pl.BlockSpec((L, w), lambda c: (c, 2)),
                  pl.BlockSpec((L, LANES), lambda c: (c, 0)),
                  full((M2_CONV, 2 * w)), full((1, 2 * w)), full((1, LANES)), full((1, LANES)),
                  full((1, w)), full((1, w)), full((LANES, w))],
        out_specs=pl.BlockSpec((L, w), lambda c: (c, 0)),
        scratch_shapes=[pltpu.VMEM((L + SUBLANES, 2 * w), F32),
                        pltpu.VMEM((L, 2 * w), F32),
                        pltpu.VMEM((M2_HEADS // 2, M2_STATE, LANES), F32),
                        pltpu.VMEM((LANES, L), F32),
                        pltpu.VMEM((L, w), F32), pltpu.VMEM((L, w), F32), pltpu.VMEM((L, w), F32)],
        compiler_params=_params("arbitrary"),
        name="mamba2_ssd",
    )(proj, proj, proj, dt_raw, conv_w, conv_b.reshape(1, -1), dtb, alog, dx, norm_g.reshape(1, -1), ex)


def _ret_kernel(q_ref, k_ref, v_ref, g_ref, ang_ref, lg_ref, o_ref, st_ref, intra_ref, qdec_ref, kdec_ref,
                cos_ref, sin_ref):
    c = pl.program_id(0)
    L = RET_CHUNK
    half = RET_QK_DIM // 2
    heads = range(RET_HEADS)

    @pl.when(c == 0)
    def _():
        st_ref[...] = jnp.zeros_like(st_ref)
        row = lax.broadcasted_iota(jnp.int32, (L, L), 0)
        col = lax.broadcasted_iota(jnp.int32, (L, L), 1)
        rel = (row - col).astype(F32)
        rowf = row.astype(F32)
        for h in heads:
            lg = lg_ref[h]
            intra_ref[h] = jnp.where(rel >= 0, jnp.exp(lg * jnp.maximum(rel, 0.0)), 0.0)
            qdec_ref[h] = jnp.exp(lg * (rowf + 1.0))
            kdec_ref[h] = jnp.exp(lg * (L - 1.0 - rowf))
        ph = lax.broadcasted_iota(jnp.int32, (L, half), 0).astype(F32) * ang_ref[...]
        cos_ref[...] = jnp.cos(ph)
        sin_ref[...] = jnp.sin(ph)

    @pl.when(c > 0)
    def _():
        step = float(L) * ang_ref[...]
        cd, sd = jnp.cos(step), jnp.sin(step)
        c0, s0 = cos_ref[...], sin_ref[...]
        cos_ref[...] = c0 * cd - s0 * sd
        sin_ref[...] = s0 * cd + c0 * sd

    cos, sin = cos_ref[...], sin_ref[...]

    def rot(x):
        x0, x1 = x[:, :half], x[:, half:]
        return jnp.concatenate([x0 * cos - x1 * sin, x1 * cos + x0 * sin], axis=1)

    qk = lambda ref, h: rot(ref[:, h * RET_QK_DIM:(h + 1) * RET_QK_DIM].astype(F32))
    vcols = [slice(h * RET_V_DIM, (h + 1) * RET_V_DIM) for h in heads]
    qbs = [qk(q_ref, h).astype(BF16) for h in heads]
    ks = [qk(k_ref, h) * (RET_QK_DIM ** -0.5) for h in heads]
    vbs = [v_ref[:, vcols[h]].astype(BF16) for h in heads]
    scores = [(_dot_nt(qbs[h], ks[h].astype(BF16)) * intra_ref[h]).astype(BF16) for h in heads]
    inters = [_dot(qbs[h], st_ref[h].astype(BF16)) for h in heads]
    for h in heads:
        kd = ks[h] * jnp.concatenate([kdec_ref[h]] * (RET_QK_DIM // LANES), axis=1)
        st_ref[h] = st_ref[h] * jnp.exp(lg_ref[h][:, 0:1] * float(L)) + _dot_tn(kd.astype(BF16), vbs[h])
    for h in heads:
        o = _dot(scores[h], vbs[h]) + jnp.concatenate([qdec_ref[h]] * (RET_V_DIM // LANES), axis=1) * inters[h]
        ms = jnp.mean(o * o, axis=-1, keepdims=True)
        o_ref[:, vcols[h]] = (o * lax.rsqrt(ms + EPS) * _silu(g_ref[:, vcols[h]].astype(F32))).astype(o_ref.dtype)


def _ret_core(proj):
    s = proj.shape[0]
    L = RET_CHUNK
    nh = RET_HEADS
    half = RET_QK_DIM // 2
    angle = (1.0 / (RET_ROT_BASE ** jnp.linspace(0.0, 1.0, half, dtype=F32))).reshape(1, half)
    log_g = jnp.log1p(-jnp.exp2(-5.0 - jnp.arange(nh, dtype=F32)))
    lg = jnp.broadcast_to(log_g[:, None, None], (nh, 1, LANES))
    return pl.pallas_call(
        _ret_kernel,
        out_shape=jax.ShapeDtypeStruct((s, RET_V_WIDTH), BF16),
        grid=(s // L,),
        in_specs=[pl.BlockSpec((L, D_MODEL), lambda c: (c, 0)),
                  pl.BlockSpec((L, D_MODEL), lambda c: (c, 1)),
                  pl.BlockSpec((L, RET_V_WIDTH), lambda c: (c, 1)),
                  pl.BlockSpec((L, RET_V_WIDTH), lambda c: (c, 2)),
                  pl.BlockSpec((1, half), lambda c: (0, 0)),
                  pl.BlockSpec((nh, 1, LANES), lambda c: (0, 0, 0))],
        out_specs=pl.BlockSpec((L, RET_V_WIDTH), lambda c: (c, 0)),
        scratch_shapes=[pltpu.VMEM((nh, RET_QK_DIM, RET_V_DIM), F32),
                        pltpu.VMEM((nh, L, L), F32), pltpu.VMEM((nh, L, LANES), F32),
                        pltpu.VMEM((nh, L, LANES), F32),
                        pltpu.VMEM((L, half), F32), pltpu.VMEM((L, half), F32)],
        compiler_params=_params("arbitrary"),
        name="retention",
    )(proj, proj, proj, proj, angle, lg)


def _ret_permute_kernel(w_ref, o_ref):
    src = lax.broadcasted_iota(jnp.int32, (RET_QK_DIM, RET_QK_DIM), 0)
    dst = lax.broadcasted_iota(jnp.int32, (RET_QK_DIM, RET_QK_DIM), 1)
    half = RET_QK_DIM // 2
    perm = jnp.where(src == 2 * (dst % half) + dst // half, 1.0, 0.0).astype(BF16)
    o_ref[...] = _dot(w_ref[...], perm).astype(o_ref.dtype)


def _ret_permute_qk(w):
    d, n = w.shape
    blk = pl.BlockSpec((d, RET_QK_DIM), lambda hb: (0, hb))
    return pl.pallas_call(
        _ret_permute_kernel,
        out_shape=jax.ShapeDtypeStruct((d, n), w.dtype),
        grid=(2 * RET_HEADS,),
        in_specs=[blk],
        out_specs=blk,
        input_output_aliases={0: 0},
        compiler_params=_params("arbitrary"),
        name="ret_permute_qk",
    )(w)


def kernel(x, norm_g, sb_w_in, sb_w_out, hg_w_in, hg_lb_logits, hg_norm_g, hg_w_out, m2_w_in, m2_conv_w,
           m2_conv_b, m2_dt_bias, m2_a_log, m2_d, m2_norm_g, m2_w_out, ret_w_in, ret_w_out, final_g):
    b, s, d = x.shape
    depth = norm_g.shape[0]
    lb_cum = jnp.cumsum(jax.nn.softmax(hg_lb_logits.astype(F32), axis=0), axis=0)
    lower_bounds = lb_cum - lb_cum[0]
    outs = []
    for bi in range(b):
        xb = x[bi]
        for i in range(depth):
            m, j = i % 4, i // 4
            fg = final_g if i == depth - 1 else None
            if m == 0:
                proj = _norm_proj(xb, norm_g[i], sb_w_in[j].astype(BF16))
                xb = _out_proj(_sb_core(proj), sb_w_out[j], xb, fg)
            elif m == 1:
                proj = _norm_proj(xb, norm_g[i], hg_w_in[j].astype(BF16))
                og = _hgrn_core(proj, lower_bounds[i], hg_norm_g[j])
                xb = _out_proj(og, hg_w_out[j], xb, fg)
            elif m == 2:
                n_main = M2_WIDTH + M2_WIDTH + 2 * M2_GROUPS * M2_STATE
                w_all = m2_w_in[j].astype(BF16)
                w_dt = jnp.pad(w_all[:, n_main:], ((0, 0), (0, LANES - M2_HEADS)))
                proj, dt_raw = _m2_in_proj(xb, norm_g[i], w_all, w_dt, n_main)
                og = _m2_core(proj, dt_raw, m2_conv_w[j], m2_conv_b[j], m2_dt_bias[j], m2_a_log[j], m2_d[j],
                              m2_norm_g[j])
                xb = _out_proj(og, m2_w_out[j], xb, fg)
            else:
                w_in = _ret_permute_qk(ret_w_in[j].astype(BF16))
                proj = _norm_proj(xb, norm_g[i], w_in)
                xb = _out_proj(_ret_core(proj), ret_w_out[j], xb, fg)
        outs.append(xb)
    return outs[0][None] if b == 1 else jnp.stack(outs, axis=0)
```

```python
import functools
import math

import jax
import jax.numpy as jnp
from jax import lax
from jax.experimental import pallas as pl
from jax.experimental.pallas import tpu as pltpu

F32 = jnp.float32
BF16 = jnp.bfloat16
EPS = 1e-6
LANES = 128
SUBLANES = 8
MXU_COLS = 256
VMEM_LIMIT = 48 * 1024 * 1024

D_MODEL = 1024
SB_HEADS, SB_HEAD_DIM, SB_BLOCK = 16, 64, 128
HG_HEADS, HG_HEAD_DIM, HG_CHUNK, HG_SUB = 8, 128, 64, 16
M2_WIDTH, M2_HEADS, M2_HEAD_DIM, M2_GROUPS, M2_STATE, M2_CONV, M2_CHUNK = 2048, 32, 64, 8, 128, 4, 128
RET_HEADS, RET_QK_DIM, RET_V_DIM, RET_V_WIDTH, RET_CHUNK = 4, 256, 512, 2048, 128
RET_ROT_BASE = 10000.0
SB_UNDERFLOW = -104.0
SB_QROWS = 64
SB_WINDOW = 2
SB_TQ = 1024


def _dot(a, b):
    return jnp.dot(a, b, preferred_element_type=F32)


def _dot_nt(a, b):
    return lax.dot_general(a, b, (((1,), (1,)), ((), ())), preferred_element_type=F32)


def _dot_tn(a, b):
    return lax.dot_general(a, b, (((0,), (0,)), ((), ())), preferred_element_type=F32)


def _split3(x):
    hi = x.astype(BF16)
    r1 = x - hi.astype(F32)
    mid = r1.astype(BF16)
    lo = (r1 - mid.astype(F32)).astype(BF16)
    return hi, mid, lo


def _dot_exact_right(x, m01x3):
    return _dot(jnp.concatenate(_split3(x), axis=1), m01x3)


def _dot_exact_left(m01, x):
    hi, mid, lo = _split3(x)
    return _dot(m01, hi) + _dot(m01, mid) + _dot(m01, lo)


def _sigmoid(x):
    return 1.0 / (1.0 + jnp.exp(-x))


def _silu(x):
    return x * _sigmoid(x)


def _softplus(x):
    u = jnp.exp(-jnp.abs(x))
    w = 1.0 + u
    log1p_u = jnp.where(w == 1.0, u, jnp.log(w) * (u / (w - 1.0)))
    return jnp.maximum(x, 0.0) + log1p_u


def _params(*sem):
    return pltpu.CompilerParams(dimension_semantics=sem, vmem_limit_bytes=VMEM_LIMIT)


def _rms_to_bf16(x_ref, g_ref, h_ref):
    xf = x_ref[...]
    ms = jnp.mean(xf * xf, axis=-1, keepdims=True)
    h_ref[...] = (xf * lax.rsqrt(ms + EPS) * g_ref[...]).astype(BF16)


def _norm_proj_kernel(x_ref, g_ref, w_ref, o_ref, h_ref):
    @pl.when(pl.program_id(1) == 0)
    def _():
        _rms_to_bf16(x_ref, g_ref, h_ref)

    o_ref[...] = _dot(h_ref[...], w_ref[...].astype(BF16)).astype(o_ref.dtype)


def _norm_proj(x, g, w, tn=2048, tm=1024):
    s, d = x.shape
    n = w.shape[1]
    tm = min(tm, s)
    return pl.pallas_call(
        _norm_proj_kernel,
        out_shape=jax.ShapeDtypeStruct((s, n), BF16),
        grid=(s // tm, n // tn),
        in_specs=[pl.BlockSpec((tm, d), lambda i, j: (i, 0)),
                  pl.BlockSpec((1, d), lambda i, j: (0, 0)),
                  pl.BlockSpec((d, tn), lambda i, j: (0, j))],
        out_specs=pl.BlockSpec((tm, tn), lambda i, j: (i, j)),
        scratch_shapes=[pltpu.VMEM((tm, d), BF16)],
        compiler_params=_params("arbitrary", "arbitrary"),
        name="norm_proj",
    )(x, g.reshape(1, d), w)


def _m2_proj_kernel(x_ref, g_ref, w_ref, dtw_ref, o_ref, dt_ref, h_ref):
    @pl.when(pl.program_id(1) == 0)
    def _():
        _rms_to_bf16(x_ref, g_ref, h_ref)
        dt_ref[...] = _dot(h_ref[...], dtw_ref[...].astype(BF16))

    o_ref[...] = _dot(h_ref[...], w_ref[...].astype(BF16)).astype(o_ref.dtype)


def _m2_in_proj(x, g, w, w_dt, n, tn=2048, tm=1024):
    s, d = x.shape
    tm = min(tm, s)
    return pl.pallas_call(
        _m2_proj_kernel,
        out_shape=(jax.ShapeDtypeStruct((s, n), BF16), jax.ShapeDtypeStruct((s, LANES), F32)),
        grid=(s // tm, n // tn),
        in_specs=[pl.BlockSpec((tm, d), lambda i, j: (i, 0)),
                  pl.BlockSpec((1, d), lambda i, j: (0, 0)),
                  pl.BlockSpec((d, tn), lambda i, j: (0, j)),
                  pl.BlockSpec((d, LANES), lambda i, j: (0, 0))],
        out_specs=(pl.BlockSpec((tm, tn), lambda i, j: (i, j)),
                   pl.BlockSpec((tm, LANES), lambda i, j: (i, 0))),
        scratch_shapes=[pltpu.VMEM((tm, d), BF16)],
        compiler_params=_params("arbitrary", "arbitrary"),
        name="m2_in_proj",
    )(x, g.reshape(1, d), w, w_dt)


def _out_proj_kernel(og_ref, w_ref, x_ref, *rest, final):
    if final:
        fg_ref, o_ref, wb_ref = rest
    else:
        o_ref, wb_ref = rest

    @pl.when(pl.program_id(0) == 0)
    def _():
        wb_ref[...] = w_ref[...].astype(BF16)

    y = x_ref[...] + _dot(og_ref[...], wb_ref[...])
    if final:
        ms = jnp.mean(y * y, axis=-1, keepdims=True)
        y = y * lax.rsqrt(ms + EPS) * fg_ref[...]
    o_ref[...] = y


def _out_proj(og, w, x, final_g=None, tm=1024):
    s, width = og.shape
    tm = min(tm, s)
    d = x.shape[1]
    in_specs = [pl.BlockSpec((tm, width), lambda i: (i, 0)),
                pl.BlockSpec((width, d), lambda i: (0, 0), pipeline_mode=pl.Buffered(1)),
                pl.BlockSpec((tm, d), lambda i: (i, 0))]
    args = [og, w, x]
    if final_g is not None:
        in_specs.append(pl.BlockSpec((1, d), lambda i: (0, 0)))
        args.append(final_g.reshape(1, d))
    return pl.pallas_call(
        functools.partial(_out_proj_kernel, final=final_g is not None),
        out_shape=jax.ShapeDtypeStruct((s, d), F32),
        grid=(s // tm,),
        in_specs=in_specs,
        out_specs=pl.BlockSpec((tm, d), lambda i: (i, 0)),
        scratch_shapes=[pltpu.VMEM((width, d), BF16)],
        compiler_params=_params("arbitrary"),
        name="out_proj",
    )(*args)


def _sb_spans(jobs, nblk, rel, k_ref, v_ref, m_rev):
    rows = jobs[0][0].shape[0]
    width = nblk * SB_BLOCK
    blk = lambda x, b: x[:, b * SB_BLOCK:(b + 1) * SB_BLOCK]
    starts = [pl.multiple_of(ks, SB_QROWS) for _, ks, _, _ in jobs]
    nzs = [_dot_nt(qs, k_ref[pl.ds(ks, width), :]) for (qs, _, _, _), ks in zip(jobs, starts)]
    masks, l1ms = [], []
    for (_, _, bound, _), nz in zip(jobs, nzs):
        mask = [rel < (bound - b * SB_BLOCK) for b in range(nblk)]
        l1m = [jnp.where(mask[b], jnp.minimum(blk(nz, b), 0.0) - jnp.log(1.0 + jnp.exp(-jnp.abs(blk(nz, b)))), 0.0)
               for b in range(nblk)]
        masks.append(mask)
        l1ms.append(l1m)
    revs = []
    for l1m in l1ms:
        stacked = jnp.concatenate(l1m, axis=0)
        hi = stacked.astype(BF16)
        lo = (stacked - hi.astype(F32)).astype(BF16)
        revs.append(_dot(jnp.concatenate([hi, lo], axis=1), m_rev))
    probs, cs = [], []
    for n, ((_, _, _, c), nz) in enumerate(zip(jobs, nzs)):
        parts = [None] * nblk
        for b in reversed(range(nblk)):
            e = jnp.exp(revs[n][b * rows:(b + 1) * rows] + c - blk(nz, b))
            parts[b] = jnp.where(masks[n][b], e, 0.0).astype(BF16)
            c = c + jnp.sum(l1ms[n][b], axis=1, keepdims=True)
        probs.append(jnp.concatenate(parts, axis=1))
        cs.append(c)
    return [(_dot(a, v_ref[pl.ds(ks, width), :]), c) for a, ks, c in zip(probs, starts, cs)]


def _sb_kernel(q_ref, k_ref, v_ref, g_ref, o_ref, acc_ref, c_ref):
    i = pl.program_id(1)
    njobs = q_ref.shape[0] // SB_QROWS
    nkeys = k_ref.shape[0]
    n_half = LANES // SB_HEAD_DIM
    rows = n_half * SB_QROWS
    span = SB_WINDOW * SB_BLOCK
    row = lax.broadcasted_iota(jnp.int32, (2 * SB_BLOCK, SB_BLOCK), 0) & (SB_BLOCK - 1)
    col = lax.broadcasted_iota(jnp.int32, (2 * SB_BLOCK, SB_BLOCK), 1)
    m_rev = jnp.where(row >= col, 1.0, 0.0).astype(BF16)
    lane_q = lax.broadcasted_iota(jnp.int32, (SB_QROWS, LANES), 1)
    hmasks = [(lane_q >= h * SB_HEAD_DIM) & (lane_q < (h + 1) * SB_HEAD_DIM) for h in range(n_half)]
    lane = lax.broadcasted_iota(jnp.int32, (rows, SB_BLOCK), 1)
    qoff = lax.broadcasted_iota(jnp.int32, (rows, SB_BLOCK), 0) & (SB_QROWS - 1)
    jobs = []
    for n in range(njobs):
        q = q_ref[n * SB_QROWS:(n + 1) * SB_QROWS, :] * (-(SB_HEAD_DIM ** -0.5))
        qs = jnp.concatenate([jnp.where(m, q, jnp.zeros_like(q)) for m in hmasks], axis=0)
        q0 = (i * njobs + n) * SB_QROWS
        ks = jnp.clip(q0 + SB_QROWS - span, 0, nkeys - span)
        jobs.append((qs, ks, q0 - ks, jnp.zeros((rows, 1), F32)))
    results = _sb_spans(jobs, SB_WINDOW, lane - qoff, k_ref, v_ref, m_rev)
    gate = g_ref[...].astype(F32)
    for n, (pv, c) in enumerate(results):
        acc_ref[n] = pv
        c_ref[n] = jnp.broadcast_to(c, c_ref.shape[1:])

    def cond(st):
        ends, cmaxes = st[:njobs], st[njobs:]
        return functools.reduce(jnp.logical_or, [jnp.logical_and(e > 0, cm > SB_UNDERFLOW)
                                                 for e, cm in zip(ends, cmaxes)])

    def body(st):
        ends = st[:njobs]
        starts = [jnp.maximum(e - SB_BLOCK, 0) for e in ends]
        step = [(qs, s0, e - s0, c_ref[n]) for n, ((qs, _, _, _), s0, e) in enumerate(zip(jobs, starts, ends))]
        new = []
        for n, (pv, cn) in enumerate(_sb_spans(step, 1, lane, k_ref, v_ref, m_rev)):
            acc_ref[n] += pv
            c_ref[n] = cn
            new.append(jnp.max(cn))
        return (*starts, *new)

    lax.while_loop(cond, body, (*[ks for _, ks, _, _ in jobs], *[jnp.max(c) for _, c in results]))
    for n in range(njobs):
        out = acc_ref[n, 0:SB_QROWS, :]
        for h in range(1, n_half):
            out = jnp.where(hmasks[h], acc_ref[n, h * SB_QROWS:(h + 1) * SB_QROWS, :], out)
        sl = slice(n * SB_QROWS, (n + 1) * SB_QROWS)
        o_ref[sl, :] = (out * _silu(gate[sl])).astype(o_ref.dtype)


def _sb_core(proj):
    s = proj.shape[0]
    nb = D_MODEL // LANES
    tq = SB_TQ
    return pl.pallas_call(
        _sb_kernel,
        out_shape=jax.ShapeDtypeStruct((s, D_MODEL), BF16),
        grid=(nb, s // tq),
        in_specs=[pl.BlockSpec((tq, LANES), lambda p, i: (i, p)),
                  pl.BlockSpec((s, LANES), lambda p, i: (0, nb + p)),
                  pl.BlockSpec((s, LANES), lambda p, i: (0, 2 * nb + p)),
                  pl.BlockSpec((tq, LANES), lambda p, i: (i, 3 * nb + p))],
        out_specs=pl.BlockSpec((tq, LANES), lambda p, i: (i, p)),
        scratch_shapes=[pltpu.VMEM((tq // SB_QROWS, LANES // SB_HEAD_DIM * SB_QROWS, LANES), F32)] * 2,
        compiler_params=_params("arbitrary", "arbitrary"),
        name="sb_attention",
    )(proj, proj, proj, proj)


def _hgrn_kernel(q_ref, f_ref, i_ref, g_ref, lb_ref, ng_ref, o_ref, st_ref, gc_ref, kc_ref, dg_ref, *, chunk):
    @pl.when(pl.program_id(1) == 0)
    def _():
        st_ref[...] = jnp.zeros_like(st_ref)

    c = chunk
    sub = HG_SUB
    nsub = c // sub
    row = lax.broadcasted_iota(jnp.int32, (c, c), 0)
    col = lax.broadcasted_iota(jnp.int32, (c, c), 1)
    tri = jnp.where(row >= col, 1.0, 0.0).astype(BF16)
    krow = lax.broadcasted_iota(jnp.int32, (c, LANES), 0)
    lb = lb_ref[...]
    nchunk = q_ref.shape[0] // c
    chunks = [slice(ci * c, (ci + 1) * c) for ci in range(nchunk)]

    q = _silu(q_ref[...].astype(F32))
    f = lb + (1.0 - lb) * _sigmoid(f_ref[...].astype(F32))
    k = 1.0 - f
    vb = i_ref[...].astype(BF16)
    logf = jnp.log2(f)
    g = jnp.concatenate([_dot_exact_left(tri, logf[ch]) for ch in chunks], axis=0)
    gc_ref[...] = g
    kc_ref[...] = k
    dg_ref[...] = jnp.zeros_like(dg_ref)
    crosses = []
    for ch in chunks:
        per_sub = [jnp.zeros((sub, c), F32)]
        for si in range(1, nsub):
            lo = ch.start + si * sub
            ref = gc_ref[lo - 1:lo, :]
            kin = jnp.where(krow < si * sub, k[ch] * jnp.exp2(ref - g[ch]), 0.0)
            qin = q[lo:lo + sub, :] * jnp.exp2(g[lo:lo + sub, :] - ref)
            per_sub.append(_dot_nt(qin.astype(BF16), kin.astype(BF16)))
        crosses.append(per_sub)
    o_intra = []
    for ch, per_sub in zip(chunks, crosses):
        for si in range(nsub):
            for rb in range(sub // SUBLANES):
                r0 = ch.start + si * sub + rb * SUBLANES
                gb, qb = g[r0:r0 + SUBLANES, :], q[r0:r0 + SUBLANES, :]
                for s in range(si * sub, si * sub + (rb + 1) * SUBLANES):
                    e = jnp.exp2(gb - gc_ref[ch.start + s:ch.start + s + 1, :])
                    dg_ref[r0:r0 + SUBLANES, s:s + 1] = jnp.sum(
                        qb * e * kc_ref[ch.start + s:ch.start + s + 1, :], axis=1, keepdims=True)
        sc = jnp.where(row >= col, jnp.where(row // sub == col // sub, dg_ref[ch, :],
                                             jnp.concatenate(per_sub, axis=0)), 0.0)
        o_intra.append(_dot(sc.astype(BF16), vb[ch]))
    qd = (q * jnp.exp2(g)).astype(BF16)
    updates, decays = [], []
    for ch in chunks:
        glast = gc_ref[ch.stop - 1:ch.stop, :]
        updates.append(_dot_tn(vb[ch], (k[ch] * jnp.exp2(glast - g[ch])).astype(BF16)))
        decays.append(jnp.exp2(glast))
    st = st_ref[...]
    outs = []
    for ci, ch in enumerate(chunks):
        outs.append(o_intra[ci] + _dot_nt(qd[ch], st.astype(BF16)))
        st = st * decays[ci] + updates[ci]
    st_ref[...] = st
    o = jnp.concatenate(outs, axis=0)
    ms = jnp.mean(o * o, axis=-1, keepdims=True)
    o_ref[...] = (o * lax.rsqrt(ms + EPS) * ng_ref[...] * _silu(g_ref[...].astype(F32))).astype(o_ref.dtype)


def _hgrn_core(proj, lb, norm_g, rows=2048, chunk=HG_CHUNK):
    s = proj.shape[0]
    rows = min(rows, s)
    nh = HG_HEADS
    blk = lambda off: pl.BlockSpec((rows, LANES), lambda h, t: (t, off * nh + h))
    vec = pl.BlockSpec((1, LANES), lambda h, t: (0, h))
    return pl.pallas_call(
        functools.partial(_hgrn_kernel, chunk=chunk),
        out_shape=jax.ShapeDtypeStruct((s, D_MODEL), BF16),
        grid=(nh, s // rows),
        in_specs=[blk(0), blk(1), blk(2), blk(3), vec, vec],
        out_specs=pl.BlockSpec((rows, LANES), lambda h, t: (t, h)),
        scratch_shapes=[pltpu.VMEM((HG_HEAD_DIM, HG_HEAD_DIM), F32),
                        pltpu.VMEM((rows, LANES), F32), pltpu.VMEM((rows, LANES), F32),
                        pltpu.VMEM((rows, chunk), F32)],
        compiler_params=_params("arbitrary", "arbitrary"),
        name="hgrn2",
    )(proj, proj, proj, proj, lb.reshape(1, -1), norm_g.reshape(1, -1))


def _m2_kernel(z_ref, xs_ref, bc_ref, dt_ref, cw_ref, cb_ref, dtb_ref, alog_ref, dx_ref, ng_ref, ex_ref,
               o_ref, xbuf_ref, act_ref, st_ref, cumt_ref, dtx_ref, cumx_ref, y_ref):
    L = M2_CHUNK
    w = M2_WIDTH
    slab = 512

    @pl.when(pl.program_id(0) == 0)
    def _():
        xbuf_ref[0:SUBLANES, :] = jnp.zeros((SUBLANES, 2 * w), F32)
        st_ref[...] = jnp.zeros_like(st_ref)

    xbuf_ref[SUBLANES:SUBLANES + L, 0:w] = xs_ref[...].astype(F32)
    xbuf_ref[SUBLANES:SUBLANES + L, w:2 * w] = bc_ref[...].astype(F32)
    sub = lax.broadcasted_iota(jnp.int32, (L // SUBLANES, SUBLANES, slab), 1)
    for c0 in range(0, 2 * w, slab):
        cols = slice(c0, c0 + slab)
        window = xbuf_ref[:, cols].reshape(L // SUBLANES + 1, SUBLANES, slab)
        acc = cb_ref[:, cols] + cw_ref[M2_CONV - 1:M2_CONV, cols] * xbuf_ref[SUBLANES:, cols]
        for back in range(1, M2_CONV):
            rot = pltpu.roll(window, back, axis=1)
            shifted = jnp.where(sub < back, rot[:-1], rot[1:]).reshape(L, slab)
            acc = acc + cw_ref[M2_CONV - 1 - back:M2_CONV - back, cols] * shifted
        act_ref[:, cols] = _silu(acc)
    xbuf_ref[0:SUBLANES, :] = xbuf_ref[L:L + SUBLANES, :]

    row = lax.broadcasted_iota(jnp.int32, (L, L), 0)
    col = lax.broadcasted_iota(jnp.int32, (L, L), 1)
    tril = row >= col
    tri = jnp.where(tril, 1.0, 0.0).astype(BF16)
    dt = _softplus(dt_ref[...] + dtb_ref[...])
    adt = dt * (-jnp.exp(alog_ref[...]))
    cum = _dot_exact_left(tri, adt)
    cumt_ref[...] = cum.T
    ex = ex_ref[...]
    dtx_ref[...] = _dot_exact_right(dt, ex)
    cumx_ref[...] = _dot_exact_right(cum, ex)
    lane_half = lax.broadcasted_iota(jnp.int32, (L, LANES), 1) // M2_HEAD_DIM

    for g in range(M2_GROUPS):
        bg = act_ref[:, w + g * M2_STATE: w + (g + 1) * M2_STATE].astype(BF16)
        cg = act_ref[:, w + (M2_GROUPS + g) * M2_STATE: w + (M2_GROUPS + g + 1) * M2_STATE].astype(BF16)
        cbm = _dot_nt(cg, bg)
        heads_per_group = M2_HEADS // M2_GROUPS
        for pp in range(heads_per_group // 2):
            pair = g * (heads_per_group // 2) + pp
            sl = slice(pair * LANES, (pair + 1) * LANES)
            xs_p = act_ref[:, sl]
            xdt = xs_p * dtx_ref[:, sl]
            xdt_b = xdt.astype(BF16)
            cumx_p = cumx_ref[:, sl]
            cum_last = cumx_ref[L - 1:L, sl]
            yd = jnp.zeros((L, LANES), F32)
            for hh in range(2):
                h = 2 * pair + hh
                cc = cumx_ref[:, h * M2_HEAD_DIM:h * M2_HEAD_DIM + 1]
                cr = cumt_ref[h:h + 1, :]
                dec = jnp.exp(jnp.where(tril, cc - cr, -jnp.inf))
                yh = _dot((cbm * dec).astype(BF16), xdt_b)
                yd = jnp.where(lane_half == hh, yh, yd)
            st = st_ref[pair]
            yoff = _dot(cg, st.astype(BF16)) * jnp.exp(cumx_p)
            dte = jnp.exp(cum_last - cumx_p)
            st_ref[pair] = st * jnp.exp(cum_last) + _dot_tn(bg, (xdt * dte).astype(BF16))
            y = yd + yoff + dx_ref[:, sl] * xs_p
            y_ref[:, sl] = y * _silu(z_ref[:, sl].astype(F32))
        gs = w // M2_GROUPS
        yg = y_ref[:, g * gs:(g + 1) * gs]
        ms = jnp.mean(yg * yg, axis=-1, keepdims=True)
        o_ref[:, g * gs:(g + 1) * gs] = (yg * lax.rsqrt(ms + EPS) * ng_ref[:, g * gs:(g + 1) * gs]).astype(o_ref.dtype)


def _m2_core(proj, dt_raw, conv_w, conv_b, dt_bias, a_log, d_skip, norm_g):
    s = proj.shape[0]
    L, w = M2_CHUNK, M2_WIDTH
    pad = LANES - M2_HEADS
    dtb = jnp.pad(dt_bias, (0, pad)).reshape(1, LANES)
    alog = jnp.pad(a_log, (0, pad)).reshape(1, LANES)
    dx = jnp.repeat(d_skip, M2_HEAD_DIM).reshape(1, w)
    ex = (jnp.arange(w)[None, :] // M2_HEAD_DIM == jnp.arange(LANES)[:, None]).astype(BF16)
    ex = jnp.tile(ex, (3, 1))
    full = lambda shape: pl.BlockSpec(shape, lambda c: (0,) * len(shape))
    return pl.pallas_call(
        _m2_kernel,
        out_shape=jax.ShapeDtypeStruct((s, w), BF16),
        grid=(s // L,),
        in_specs=[pl.BlockSpec((L, w), lambda c: (c, 0)),
                  pl.BlockSpec((L, w), lambda c: (c, 1)),
                  pl.BlockSpec((L, w), lambda c: (c, 2)),
                  pl.BlockSpec((L, LANES), lambda c: (c, 0)),
                  full((M2_CONV, 2 * w)), full((1, 2 * w)), full((1, LANES)), full((1, LANES)),
                  full((1, w)), full((1, w)), full((3 * LANES, w))],
        out_specs=pl.BlockSpec((L, w), lambda c: (c, 0)),
        scratch_shapes=[pltpu.VMEM((L + SUBLANES, 2 * w), F32),
                        pltpu.VMEM((L, 2 * w), F32),
                        pltpu.VMEM((M2_HEADS // 2, M2_STATE, LANES), F32),
                        pltpu.VMEM((LANES, L), F32),
                        pltpu.VMEM((L, w), F32), pltpu.VMEM((L, w), F32), pltpu.VMEM((L, w), F32)],
        compiler_params=_params("arbitrary"),
        name="mamba2_ssd",
    )(proj, proj, proj, dt_raw, conv_w, conv_b.reshape(1, -1), dtb, alog, dx, norm_g.reshape(1, -1), ex)


def _ret_kernel(q_ref, k_ref, v_ref, g_ref, ang_ref, lg_ref, o_ref, st_ref, intra_ref, qdec_ref, kdec_ref,
                cos_ref, sin_ref):
    c = pl.program_id(0)
    L = RET_CHUNK
    half = RET_QK_DIM // 2
    heads = range(RET_HEADS)

    @pl.when(c == 0)
    def _():
        st_ref[...] = jnp.zeros_like(st_ref)
        row = lax.broadcasted_iota(jnp.int32, (L, L), 0)
        col = lax.broadcasted_iota(jnp.int32, (L, L), 1)
        rel = (row - col).astype(F32)
        rowf = row.astype(F32)
        for h in heads:
            lg = lg_ref[h]
            intra_ref[h] = jnp.where(rel >= 0, jnp.exp(lg * jnp.maximum(rel, 0.0)), 0.0)
            qdec_ref[h] = jnp.exp(lg * (rowf + 1.0))
            kdec_ref[h] = jnp.exp(lg * (L - 1.0 - rowf))
        ph = lax.broadcasted_iota(jnp.int32, (L, half), 0).astype(F32) * ang_ref[...]
        cos_ref[...] = jnp.cos(ph)
        sin_ref[...] = jnp.sin(ph)

    @pl.when(c > 0)
    def _():
        step = float(L) * ang_ref[...]
        cd, sd = jnp.cos(step), jnp.sin(step)
        c0, s0 = cos_ref[...], sin_ref[...]
        cos_ref[...] = c0 * cd - s0 * sd
        sin_ref[...] = s0 * cd + c0 * sd

    cos, sin = cos_ref[...], sin_ref[...]

    def rot(x):
        x0, x1 = x[:, :half], x[:, half:]
        return jnp.concatenate([x0 * cos - x1 * sin, x1 * cos + x0 * sin], axis=1)

    qk = lambda ref, h: rot(ref[:, h * RET_QK_DIM:(h + 1) * RET_QK_DIM].astype(F32))
    vcols = [slice(h * RET_V_DIM, (h + 1) * RET_V_DIM) for h in heads]
    qbs = [qk(q_ref, h).astype(BF16) for h in heads]
    ks = [qk(k_ref, h) * (RET_QK_DIM ** -0.5) for h in heads]
    vbs = [v_ref[:, vcols[h]].astype(BF16) for h in heads]
    scores = [(_dot_nt(qbs[h], ks[h].astype(BF16)) * intra_ref[h]).astype(BF16) for h in heads]
    inters = [_dot(qbs[h], st_ref[h].astype(BF16)) for h in heads]
    for h in heads:
        kd = ks[h] * jnp.concatenate([kdec_ref[h]] * (RET_QK_DIM // LANES), axis=1)
        st_ref[h] = st_ref[h] * jnp.exp(lg_ref[h][:, 0:1] * float(L)) + _dot_tn(kd.astype(BF16), vbs[h])
    for h in heads:
        o = _dot(scores[h], vbs[h]) + jnp.concatenate([qdec_ref[h]] * (RET_V_DIM // LANES), axis=1) * inters[h]
        ms = jnp.mean(o * o, axis=-1, keepdims=True)
        o_ref[:, vcols[h]] = (o * lax.rsqrt(ms + EPS) * _silu(g_ref[:, vcols[h]].astype(F32))).astype(o_ref.dtype)


def _ret_core(proj):
    s = proj.shape[0]
    L = RET_CHUNK
    nh = RET_HEADS
    half = RET_QK_DIM // 2
    angle = (1.0 / (RET_ROT_BASE ** jnp.linspace(0.0, 1.0, half, dtype=F32))).reshape(1, half)
    log_g = jnp.log1p(-jnp.exp2(-5.0 - jnp.arange(nh, dtype=F32)))
    lg = jnp.broadcast_to(log_g[:, None, None], (nh, 1, LANES))
    return pl.pallas_call(
        _ret_kernel,
        out_shape=jax.ShapeDtypeStruct((s, RET_V_WIDTH), BF16),
        grid=(s // L,),
        in_specs=[pl.BlockSpec((L, D_MODEL), lambda c: (c, 0)),
                  pl.BlockSpec((L, D_MODEL), lambda c: (c, 1)),
                  pl.BlockSpec((L, RET_V_WIDTH), lambda c: (c, 1)),
                  pl.BlockSpec((L, RET_V_WIDTH), lambda c: (c, 2)),
                  pl.BlockSpec((1, half), lambda c: (0, 0)),
                  pl.BlockSpec((nh, 1, LANES), lambda c: (0, 0, 0))],
        out_specs=pl.BlockSpec((L, RET_V_WIDTH), lambda c: (c, 0)),
        scratch_shapes=[pltpu.VMEM((nh, RET_QK_DIM, RET_V_DIM), F32),
                        pltpu.VMEM((nh, L, L), F32), pltpu.VMEM((nh, L, LANES), F32),
                        pltpu.VMEM((nh, L, LANES), F32),
                        pltpu.VMEM((L, half), F32), pltpu.VMEM((L, half), F32)],
        compiler_params=_params("arbitrary"),
        name="retention",
    )(proj, proj, proj, proj, angle, lg)


def _ret_permute_kernel(w_ref, o_ref):
    src = lax.broadcasted_iota(jnp.int32, (RET_QK_DIM, RET_QK_DIM), 0)
    dst = lax.broadcasted_iota(jnp.int32, (RET_QK_DIM, RET_QK_DIM), 1)
    half = RET_QK_DIM // 2
    perm = jnp.where(src == 2 * (dst % half) + dst // half, 1.0, 0.0).astype(BF16)
    o_ref[...] = _dot(w_ref[...], perm).astype(o_ref.dtype)


def _ret_permute_qk(w):
    d, n = w.shape
    blk = pl.BlockSpec((d, RET_QK_DIM), lambda hb: (0, hb))
    return pl.pallas_call(
        _ret_permute_kernel,
        out_shape=jax.ShapeDtypeStruct((d, n), w.dtype),
        grid=(2 * RET_HEADS,),
        in_specs=[blk],
        out_specs=blk,
        input_output_aliases={0: 0},
        compiler_params=_params("arbitrary"),
        name="ret_permute_qk",
    )(w)


def kernel(x, norm_g, sb_w_in, sb_w_out, hg_w_in, hg_lb_logits, hg_norm_g, hg_w_out, m2_w_in, m2_conv_w,
           m2_conv_b, m2_dt_bias, m2_a_log, m2_d, m2_norm_g, m2_w_out, ret_w_in, ret_w_out, final_g):
    b, s, d = x.shape
    depth = norm_g.shape[0]
    lb_cum = jnp.cumsum(jax.nn.softmax(hg_lb_logits.astype(F32), axis=0), axis=0)
    lower_bounds = lb_cum - lb_cum[0]
    outs = []
    for bi in range(b):
        xb = x[bi]
        for i in range(depth):
            m, j = i % 4, i // 4
            fg = final_g if i == depth - 1 else None
            if m == 0:
                proj = _norm_proj(xb, norm_g[i], sb_w_in[j].astype(BF16))
                xb = _out_proj(_sb_core(proj), sb_w_out[j], xb, fg)
            elif m == 1:
                proj = _norm_proj(xb, norm_g[i], hg_w_in[j].astype(BF16))
                og = _hgrn_core(proj, lower_bounds[i], hg_norm_g[j])
                xb = _out_proj(og, hg_w_out[j], xb, fg)
            elif m == 2:
                n_main = M2_WIDTH + M2_WIDTH + 2 * M2_GROUPS * M2_STATE
                w_all = m2_w_in[j].astype(BF16)
                w_dt = jnp.pad(w_all[:, n_main:], ((0, 0), (0, LANES - M2_HEADS)))
                proj, dt_raw = _m2_in_proj(xb, norm_g[i], w_all, w_dt, n_main)
                og = _m2_core(proj, dt_raw, m2_conv_w[j], m2_conv_b[j], m2_dt_bias[j], m2_a_log[j], m2_d[j],
                              m2_norm_g[j])
                xb = _out_proj(og, m2_w_out[j], xb, fg)
            else:
                w_in = _ret_permute_qk(ret_w_in[j].astype(BF16))
                proj = _norm_proj(xb, norm_g[i], w_in)
                xb = _out_proj(_ret_core(proj), ret_w_out[j], xb, fg)
        outs.append(xb)
    return outs[0][None] if b == 1 else jnp.stack(outs, axis=0)
```

```python
import functools
import math

import jax
import jax.numpy as jnp
from jax import lax
from jax.experimental import pallas as pl
from jax.experimental.pallas import tpu as pltpu

F32 = jnp.float32
BF16 = jnp.bfloat16
EPS = 1e-6
LANES = 128
SUBLANES = 8
MXU_COLS = 256
VMEM_LIMIT = 48 * 1024 * 1024

D_MODEL = 1024
SB_HEADS, SB_HEAD_DIM, SB_BLOCK = 16, 64, 128
HG_HEADS, HG_HEAD_DIM, HG_CHUNK, HG_SUB = 8, 128, 64, 16
M2_WIDTH, M2_HEADS, M2_HEAD_DIM, M2_GROUPS, M2_STATE, M2_CONV, M2_CHUNK = 2048, 32, 64, 8, 128, 4, 128
RET_HEADS, RET_QK_DIM, RET_V_DIM, RET_V_WIDTH, RET_CHUNK = 4, 256, 512, 2048, 128
RET_ROT_BASE = 10000.0
SB_UNDERFLOW = -104.0
SB_QROWS = 64
SB_WINDOW = 2
SB_TQ = 2048


def _dot(a, b):
    return jnp.dot(a, b, preferred_element_type=F32)


def _dot_nt(a, b):
    return lax.dot_general(a, b, (((1,), (1,)), ((), ())), preferred_element_type=F32)


def _dot_tn(a, b):
    return lax.dot_general(a, b, (((0,), (0,)), ((), ())), preferred_element_type=F32)


def _split3(x):
    hi = x.astype(BF16)
    r1 = x - hi.astype(F32)
    mid = r1.astype(BF16)
    lo = (r1 - mid.astype(F32)).astype(BF16)
    return hi, mid, lo


def _dot_exact_right(x, m01x3):
    return _dot(jnp.concatenate(_split3(x), axis=1), m01x3)


def _dot_exact_left(m01, x):
    hi, mid, lo = _split3(x)
    return _dot(m01, hi) + _dot(m01, mid) + _dot(m01, lo)


def _sigmoid(x):
    return 1.0 / (1.0 + jnp.exp(-x))


def _silu(x):
    return x * _sigmoid(x)


def _softplus(x):
    u = jnp.exp(-jnp.abs(x))
    w = 1.0 + u
    log1p_u = jnp.where(w == 1.0, u, jnp.log(w) * (u / (w - 1.0)))
    return jnp.maximum(x, 0.0) + log1p_u


def _params(*sem):
    return pltpu.CompilerParams(dimension_semantics=sem, vmem_limit_bytes=VMEM_LIMIT)


def _rms_to_bf16(x_ref, g_ref, h_ref):
    xf = x_ref[...]
    ms = jnp.mean(xf * xf, axis=-1, keepdims=True)
    h_ref[...] = (xf * lax.rsqrt(ms + EPS) * g_ref[...]).astype(BF16)


def _norm_proj_kernel(x_ref, g_ref, w_ref, o_ref, h_ref):
    @pl.when(pl.program_id(1) == 0)
    def _():
        _rms_to_bf16(x_ref, g_ref, h_ref)

    o_ref[...] = _dot(h_ref[...], w_ref[...].astype(BF16)).astype(o_ref.dtype)


def _norm_proj(x, g, w, tn=2048, tm=1024):
    s, d = x.shape
    n = w.shape[1]
    tm = min(tm, s)
    return pl.pallas_call(
        _norm_proj_kernel,
        out_shape=jax.ShapeDtypeStruct((s, n), BF16),
        grid=(s // tm, n // tn),
        in_specs=[pl.BlockSpec((tm, d), lambda i, j: (i, 0)),
                  pl.BlockSpec((1, d), lambda i, j: (0, 0)),
                  pl.BlockSpec((d, tn), lambda i, j: (0, j))],
        out_specs=pl.BlockSpec((tm, tn), lambda i, j: (i, j)),
        scratch_shapes=[pltpu.VMEM((tm, d), BF16)],
        compiler_params=_params("arbitrary", "arbitrary"),
        name="norm_proj",
    )(x, g.reshape(1, d), w)


def _m2_proj_kernel(x_ref, g_ref, w_ref, dtw_ref, o_ref, dt_ref, h_ref):
    @pl.when(pl.program_id(1) == 0)
    def _():
        _rms_to_bf16(x_ref, g_ref, h_ref)
        dt_ref[...] = _dot(h_ref[...], dtw_ref[...].astype(BF16))

    o_ref[...] = _dot(h_ref[...], w_ref[...].astype(BF16)).astype(o_ref.dtype)


def _m2_in_proj(x, g, w, w_dt, n, tn=2048, tm=1024):
    s, d = x.shape
    tm = min(tm, s)
    return pl.pallas_call(
        _m2_proj_kernel,
        out_shape=(jax.ShapeDtypeStruct((s, n), BF16), jax.ShapeDtypeStruct((s, LANES), F32)),
        grid=(s // tm, n // tn),
        in_specs=[pl.BlockSpec((tm, d), lambda i, j: (i, 0)),
                  pl.BlockSpec((1, d), lambda i, j: (0, 0)),
                  pl.BlockSpec((d, tn), lambda i, j: (0, j)),
                  pl.BlockSpec((d, LANES), lambda i, j: (0, 0))],
        out_specs=(pl.BlockSpec((tm, tn), lambda i, j: (i, j)),
                   pl.BlockSpec((tm, LANES), lambda i, j: (i, 0))),
        scratch_shapes=[pltpu.VMEM((tm, d), BF16)],
        compiler_params=_params("arbitrary", "arbitrary"),
        name="m2_in_proj",
    )(x, g.reshape(1, d), w, w_dt)


def _out_proj_kernel(og_ref, w_ref, x_ref, *rest, final):
    if final:
        fg_ref, o_ref, wb_ref = rest
    else:
        o_ref, wb_ref = rest

    @pl.when(pl.program_id(0) == 0)
    def _():
        wb_ref[...] = w_ref[...].astype(BF16)

    y = x_ref[...] + _dot(og_ref[...], wb_ref[...])
    if final:
        ms = jnp.mean(y * y, axis=-1, keepdims=True)
        y = y * lax.rsqrt(ms + EPS) * fg_ref[...]
    o_ref[...] = y


def _out_proj(og, w, x, final_g=None, tm=1024):
    s, width = og.shape
    tm = min(tm, s)
    d = x.shape[1]
    in_specs = [pl.BlockSpec((tm, width), lambda i: (i, 0)),
                pl.BlockSpec((width, d), lambda i: (0, 0), pipeline_mode=pl.Buffered(1)),
                pl.BlockSpec((tm, d), lambda i: (i, 0))]
    args = [og, w, x]
    if final_g is not None:
        in_specs.append(pl.BlockSpec((1, d), lambda i: (0, 0)))
        args.append(final_g.reshape(1, d))
    return pl.pallas_call(
        functools.partial(_out_proj_kernel, final=final_g is not None),
        out_shape=jax.ShapeDtypeStruct((s, d), F32),
        grid=(s // tm,),
        in_specs=in_specs,
        out_specs=pl.BlockSpec((tm, d), lambda i: (i, 0)),
        scratch_shapes=[pltpu.VMEM((width, d), BF16)],
        compiler_params=_params("arbitrary"),
        name="out_proj",
    )(*args)


def _sb_spans(jobs, nblk, rel, k_ref, v_ref, m_rev):
    rows = jobs[0][0].shape[0]
    width = nblk * SB_BLOCK
    blk = lambda x, b: x[:, b * SB_BLOCK:(b + 1) * SB_BLOCK]
    starts = [pl.multiple_of(ks, SB_QROWS) for _, ks, _, _ in jobs]
    nzs = [_dot_nt(qs, k_ref[pl.ds(ks, width), :]) for (qs, _, _, _), ks in zip(jobs, starts)]
    masks, l1ms = [], []
    for (_, _, bound, _), nz in zip(jobs, nzs):
        mask = [rel < (bound - b * SB_BLOCK) for b in range(nblk)]
        l1m = [jnp.where(mask[b], jnp.minimum(blk(nz, b), 0.0) - jnp.log(1.0 + jnp.exp(-jnp.abs(blk(nz, b)))), 0.0)
               for b in range(nblk)]
        masks.append(mask)
        l1ms.append(l1m)
    revs = []
    for l1m in l1ms:
        stacked = jnp.concatenate(l1m, axis=0)
        hi = stacked.astype(BF16)
        lo = (stacked - hi.astype(F32)).astype(BF16)
        revs.append(_dot(jnp.concatenate([hi, lo], axis=1), m_rev))
    probs, cs = [], []
    for n, ((_, _, _, c), nz) in enumerate(zip(jobs, nzs)):
        parts = [None] * nblk
        for b in reversed(range(nblk)):
            e = jnp.exp(revs[n][b * rows:(b + 1) * rows] + c - blk(nz, b))
            parts[b] = jnp.where(masks[n][b], e, 0.0).astype(BF16)
            c = c + jnp.sum(l1ms[n][b], axis=1, keepdims=True)
        probs.append(jnp.concatenate(parts, axis=1))
        cs.append(c)
    return [(_dot(a, v_ref[pl.ds(ks, width), :]), c) for a, ks, c in zip(probs, starts, cs)]


def _sb_kernel(q_ref, k_ref, v_ref, g_ref, o_ref, acc_ref, c_ref):
    i = pl.program_id(1)
    njobs = q_ref.shape[0] // SB_QROWS
    nkeys = k_ref.shape[0]
    n_half = LANES // SB_HEAD_DIM
    rows = n_half * SB_QROWS
    span = SB_WINDOW * SB_BLOCK
    row = lax.broadcasted_iota(jnp.int32, (2 * SB_BLOCK, SB_BLOCK), 0) & (SB_BLOCK - 1)
    col = lax.broadcasted_iota(jnp.int32, (2 * SB_BLOCK, SB_BLOCK), 1)
    m_rev = jnp.where(row >= col, 1.0, 0.0).astype(BF16)
    lane_q = lax.broadcasted_iota(jnp.int32, (SB_QROWS, LANES), 1)
    hmasks = [(lane_q >= h * SB_HEAD_DIM) & (lane_q < (h + 1) * SB_HEAD_DIM) for h in range(n_half)]
    lane = lax.broadcasted_iota(jnp.int32, (rows, SB_BLOCK), 1)
    qoff = lax.broadcasted_iota(jnp.int32, (rows, SB_BLOCK), 0) & (SB_QROWS - 1)
    jobs = []
    for n in range(njobs):
        q = q_ref[n * SB_QROWS:(n + 1) * SB_QROWS, :] * (-(SB_HEAD_DIM ** -0.5))
        qs = jnp.concatenate([jnp.where(m, q, jnp.zeros_like(q)) for m in hmasks], axis=0)
        q0 = (i * njobs + n) * SB_QROWS
        ks = jnp.clip(q0 + SB_QROWS - span, 0, nkeys - span)
        jobs.append((qs, ks, q0 - ks, jnp.zeros((rows, 1), F32)))
    results = _sb_spans(jobs, SB_WINDOW, lane - qoff, k_ref, v_ref, m_rev)
    gate = g_ref[...].astype(F32)
    for n, (pv, c) in enumerate(results):
        acc_ref[n] = pv
        c_ref[n] = jnp.broadcast_to(c, c_ref.shape[1:])

    def cond(st):
        ends, cmaxes = st[:njobs], st[njobs:]
        return functools.reduce(jnp.logical_or, [jnp.logical_and(e > 0, cm > SB_UNDERFLOW)
                                                 for e, cm in zip(ends, cmaxes)])

    def body(st):
        ends = st[:njobs]
        starts = [jnp.maximum(e - SB_BLOCK, 0) for e in ends]
        step = [(qs, s0, e - s0, c_ref[n]) for n, ((qs, _, _, _), s0, e) in enumerate(zip(jobs, starts, ends))]
        new = []
        for n, (pv, cn) in enumerate(_sb_spans(step, 1, lane, k_ref, v_ref, m_rev)):
            acc_ref[n] += pv
            c_ref[n] = cn
            new.append(jnp.max(cn))
        return (*starts, *new)

    lax.while_loop(cond, body, (*[ks for _, ks, _, _ in jobs], *[jnp.max(c) for _, c in results]))
    for n in range(njobs):
        out = acc_ref[n, 0:SB_QROWS, :]
        for h in range(1, n_half):
            out = jnp.where(hmasks[h], acc_ref[n, h * SB_QROWS:(h + 1) * SB_QROWS, :], out)
        sl = slice(n * SB_QROWS, (n + 1) * SB_QROWS)
        o_ref[sl, :] = (out * _silu(gate[sl])).astype(o_ref.dtype)


def _sb_core(proj):
    s = proj.shape[0]
    nb = D_MODEL // LANES
    tq = SB_TQ
    return pl.pallas_call(
        _sb_kernel,
        out_shape=jax.ShapeDtypeStruct((s, D_MODEL), BF16),
        grid=(nb, s // tq),
        in_specs=[pl.BlockSpec((tq, LANES), lambda p, i: (i, p)),
                  pl.BlockSpec((s, LANES), lambda p, i: (0, nb + p)),
                  pl.BlockSpec((s, LANES), lambda p, i: (0, 2 * nb + p)),
                  pl.BlockSpec((tq, LANES), lambda p, i: (i, 3 * nb + p))],
        out_specs=pl.BlockSpec((tq, LANES), lambda p, i: (i, p)),
        scratch_shapes=[pltpu.VMEM((tq // SB_QROWS, LANES // SB_HEAD_DIM * SB_QROWS, LANES), F32)] * 2,
        compiler_params=_params("arbitrary", "arbitrary"),
        name="sb_attention",
    )(proj, proj, proj, proj)


def _hgrn_kernel(q_ref, f_ref, i_ref, g_ref, lb_ref, ng_ref, o_ref, st_ref, gc_ref, kc_ref, dg_ref, *, chunk):
    @pl.when(pl.program_id(1) == 0)
    def _():
        st_ref[...] = jnp.zeros_like(st_ref)

    c = chunk
    sub = HG_SUB
    nsub = c // sub
    row = lax.broadcasted_iota(jnp.int32, (c, c), 0)
    col = lax.broadcasted_iota(jnp.int32, (c, c), 1)
    tri = jnp.where(row >= col, 1.0, 0.0).astype(BF16)
    krow = lax.broadcasted_iota(jnp.int32, (c, LANES), 0)
    lb = lb_ref[...]
    nchunk = q_ref.shape[0] // c
    chunks = [slice(ci * c, (ci + 1) * c) for ci in range(nchunk)]

    q = _silu(q_ref[...].astype(F32))
    f = lb + (1.0 - lb) * _sigmoid(f_ref[...].astype(F32))
    k = 1.0 - f
    vb = i_ref[...].astype(BF16)
    logf = jnp.log2(f)
    g = jnp.concatenate([_dot_exact_left(tri, logf[ch]) for ch in chunks], axis=0)
    gc_ref[...] = g
    kc_ref[...] = k
    dg_ref[...] = jnp.zeros_like(dg_ref)
    crosses = []
    for ch in chunks:
        per_sub = [jnp.zeros((sub, c), F32)]
        for si in range(1, nsub):
            lo = ch.start + si * sub
            ref = gc_ref[lo - 1:lo, :]
            kin = jnp.where(krow < si * sub, k[ch] * jnp.exp2(ref - g[ch]), 0.0)
            qin = q[lo:lo + sub, :] * jnp.exp2(g[lo:lo + sub, :] - ref)
            per_sub.append(_dot_nt(qin.astype(BF16), kin.astype(BF16)))
        crosses.append(per_sub)
    o_intra = []
    for ch, per_sub in zip(chunks, crosses):
        for si in range(nsub):
            for rb in range(sub // SUBLANES):
                r0 = ch.start + si * sub + rb * SUBLANES
                gb, qb = g[r0:r0 + SUBLANES, :], q[r0:r0 + SUBLANES, :]
                for s in range(si * sub, si * sub + (rb + 1) * SUBLANES):
                    e = jnp.exp2(gb - gc_ref[ch.start + s:ch.start + s + 1, :])
                    dg_ref[r0:r0 + SUBLANES, s:s + 1] = jnp.sum(
                        qb * e * kc_ref[ch.start + s:ch.start + s + 1, :], axis=1, keepdims=True)
        sc = jnp.where(row >= col, jnp.where(row // sub == col // sub, dg_ref[ch, :],
                                             jnp.concatenate(per_sub, axis=0)), 0.0)
        o_intra.append(_dot(sc.astype(BF16), vb[ch]))
    qd = (q * jnp.exp2(g)).astype(BF16)
    updates, decays = [], []
    for ch in chunks:
        glast = gc_ref[ch.stop - 1:ch.stop, :]
        updates.append(_dot_tn(vb[ch], (k[ch] * jnp.exp2(glast - g[ch])).astype(BF16)))
        decays.append(jnp.exp2(glast))
    st = st_ref[...]
    outs = []
    for ci, ch in enumerate(chunks):
        outs.append(o_intra[ci] + _dot_nt(qd[ch], st.astype(BF16)))
        st = st * decays[ci] + updates[ci]
    st_ref[...] = st
    o = jnp.concatenate(outs, axis=0)
    ms = jnp.mean(o * o, axis=-1, keepdims=True)
    o_ref[...] = (o * lax.rsqrt(ms + EPS) * ng_ref[...] * _silu(g_ref[...].astype(F32))).astype(o_ref.dtype)


def _hgrn_core(proj, lb, norm_g, rows=2048, chunk=HG_CHUNK):
    s = proj.shape[0]
    rows = min(rows, s)
    nh = HG_HEADS
    blk = lambda off: pl.BlockSpec((rows, LANES), lambda h, t: (t, off * nh + h))
    vec = pl.BlockSpec((1, LANES), lambda h, t: (0, h))
    return pl.pallas_call(
        functools.partial(_hgrn_kernel, chunk=chunk),
        out_shape=jax.ShapeDtypeStruct((s, D_MODEL), BF16),
        grid=(nh, s // rows),
        in_specs=[blk(0), blk(1), blk(2), blk(3), vec, vec],
        out_specs=pl.BlockSpec((rows, LANES), lambda h, t: (t, h)),
        scratch_shapes=[pltpu.VMEM((HG_HEAD_DIM, HG_HEAD_DIM), F32),
                        pltpu.VMEM((rows, LANES), F32), pltpu.VMEM((rows, LANES), F32),
                        pltpu.VMEM((rows, chunk), F32)],
        compiler_params=_params("arbitrary", "arbitrary"),
        name="hgrn2",
    )(proj, proj, proj, proj, lb.reshape(1, -1), norm_g.reshape(1, -1))


def _m2_kernel(z_ref, xs_ref, bc_ref, dt_ref, cw_ref, cb_ref, dtb_ref, alog_ref, dx_ref, ng_ref, ex_ref,
               o_ref, xbuf_ref, act_ref, st_ref, cumt_ref, dtx_ref, cumx_ref, y_ref):
    L = M2_CHUNK
    w = M2_WIDTH
    slab = 512

    @pl.when(pl.program_id(0) == 0)
    def _():
        xbuf_ref[0:SUBLANES, :] = jnp.zeros((SUBLANES, 2 * w), F32)
        st_ref[...] = jnp.zeros_like(st_ref)

    xbuf_ref[SUBLANES:SUBLANES + L, 0:w] = xs_ref[...].astype(F32)
    xbuf_ref[SUBLANES:SUBLANES + L, w:2 * w] = bc_ref[...].astype(F32)
    sub = lax.broadcasted_iota(jnp.int32, (L // SUBLANES, SUBLANES, slab), 1)
    for c0 in range(0, 2 * w, slab):
        cols = slice(c0, c0 + slab)
        window = xbuf_ref[:, cols].reshape(L // SUBLANES + 1, SUBLANES, slab)
        acc = cb_ref[:, cols] + cw_ref[M2_CONV - 1:M2_CONV, cols] * xbuf_ref[SUBLANES:, cols]
        for back in range(1, M2_CONV):
            rot = pltpu.roll(window, back, axis=1)
            shifted = jnp.where(sub < back, rot[:-1], rot[1:]).reshape(L, slab)
            acc = acc + cw_ref[M2_CONV - 1 - back:M2_CONV - back, cols] * shifted
        act_ref[:, cols] = _silu(acc)
    xbuf_ref[0:SUBLANES, :] = xbuf_ref[L:L + SUBLANES, :]

    row = lax.broadcasted_iota(jnp.int32, (L, L), 0)
    col = lax.broadcasted_iota(jnp.int32, (L, L), 1)
    tril = row >= col
    tri = jnp.where(tril, 1.0, 0.0).astype(BF16)
    dt = _softplus(dt_ref[...] + dtb_ref[...])
    adt = dt * (-jnp.exp(alog_ref[...]))
    cum = _dot_exact_left(tri, adt)
    cumt_ref[...] = cum.T
    ex = ex_ref[...]
    dtx_ref[...] = _dot_exact_right(dt, ex)
    cumx_ref[...] = _dot_exact_right(cum, ex)
    lane_half = lax.broadcasted_iota(jnp.int32, (L, LANES), 1) // M2_HEAD_DIM

    for g in range(M2_GROUPS):
        bg = act_ref[:, w + g * M2_STATE: w + (g + 1) * M2_STATE].astype(BF16)
        cg = act_ref[:, w + (M2_GROUPS + g) * M2_STATE: w + (M2_GROUPS + g + 1) * M2_STATE].astype(BF16)
        cbm = _dot_nt(cg, bg)
        heads_per_group = M2_HEADS // M2_GROUPS
        for pp in range(heads_per_group // 2):
            pair = g * (heads_per_group // 2) + pp
            sl = slice(pair * LANES, (pair + 1) * LANES)
            xs_p = act_ref[:, sl]
            xdt = xs_p * dtx_ref[:, sl]
            xdt_b = xdt.astype(BF16)
            cumx_p = cumx_ref[:, sl]
            cum_last = cumx_ref[L - 1:L, sl]
            yd = jnp.zeros((L, LANES), F32)
            for hh in range(2):
                h = 2 * pair + hh
                cc = cumx_ref[:, h * M2_HEAD_DIM:h * M2_HEAD_DIM + 1]
                cr = cumt_ref[h:h + 1, :]
                dec = jnp.exp(jnp.where(tril, cc - cr, -jnp.inf))
                yh = _dot((cbm * dec).astype(BF16), xdt_b)
                yd = jnp.where(lane_half == hh, yh, yd)
            st = st_ref[pair]
            yoff = _dot(cg, st.astype(BF16)) * jnp.exp(cumx_p)
            dte = jnp.exp(cum_last - cumx_p)
            st_ref[pair] = st * jnp.exp(cum_last) + _dot_tn(bg, (xdt * dte).astype(BF16))
            y = yd + yoff + dx_ref[:, sl] * xs_p
            y_ref[:, sl] = y * _silu(z_ref[:, sl].astype(F32))
        gs = w // M2_GROUPS
        yg = y_ref[:, g * gs:(g + 1) * gs]
        ms = jnp.mean(yg * yg, axis=-1, keepdims=True)
        o_ref[:, g * gs:(g + 1) * gs] = (yg * lax.rsqrt(ms + EPS) * ng_ref[:, g * gs:(g + 1) * gs]).astype(o_ref.dtype)


def _m2_core(proj, dt_raw, conv_w, conv_b, dt_bias, a_log, d_skip, norm_g):
    s = proj.shape[0]
    L, w = M2_CHUNK, M2_WIDTH
    pad = LANES - M2_HEADS
    dtb = jnp.pad(dt_bias, (0, pad)).reshape(1, LANES)
    alog = jnp.pad(a_log, (0, pad)).reshape(1, LANES)
    dx = jnp.repeat(d_skip, M2_HEAD_DIM).reshape(1, w)
    ex = (jnp.arange(w)[None, :] // M2_HEAD_DIM == jnp.arange(LANES)[:, None]).astype(BF16)
    ex = jnp.tile(ex, (3, 1))
    full = lambda shape: pl.BlockSpec(shape, lambda c: (0,) * len(shape))
    return pl.pallas_call(
        _m2_kernel,
        out_shape=jax.ShapeDtypeStruct((s, w), BF16),
        grid=(s // L,),
        in_specs=[pl.BlockSpec((L, w), lambda c: (c, 0)),
                  pl.BlockSpec((L, w), lambda c: (c, 1)),
                  pl.BlockSpec((L, w), lambda c: (c, 2)),
                  pl.BlockSpec((L, LANES), lambda c: (c, 0)),
                  full((M2_CONV, 2 * w)), full((1, 2 * w)), full((1, LANES)), full((1, LANES)),
                  full((1, w)), full((1, w)), full((3 * LANES, w))],
        out_specs=pl.BlockSpec((L, w), lambda c: (c, 0)),
        scratch_shapes=[pltpu.VMEM((L + SUBLANES, 2 * w), F32),
                        pltpu.VMEM((L, 2 * w), F32),
                        pltpu.VMEM((M2_HEADS // 2, M2_STATE, LANES), F32),
                        pltpu.VMEM((LANES, L), F32),
                        pltpu.VMEM((L, w), F32), pltpu.VMEM((L, w), F32), pltpu.VMEM((L, w), F32)],
        compiler_params=_params("arbitrary"),
        name="mamba2_ssd",
    )(proj, proj, proj, dt_raw, conv_w, conv_b.reshape(1, -1), dtb, alog, dx, norm_g.reshape(1, -1), ex)


def _ret_kernel(q_ref, k_ref, v_ref, g_ref, ang_ref, lg_ref, o_ref, st_ref, intra_ref, qdec_ref, kdec_ref,
                cos_ref, sin_ref):
    c = pl.program_id(0)
    L = RET_CHUNK
    half = RET_QK_DIM // 2
    heads = range(RET_HEADS)

    @pl.when(c == 0)
    def _():
        st_ref[...] = jnp.zeros_like(st_ref)
        row = lax.broadcasted_iota(jnp.int32, (L, L), 0)
        col = lax.broadcasted_iota(jnp.int32, (L, L), 1)
        rel = (row - col).astype(F32)
        rowf = row.astype(F32)
        for h in heads:
            lg = lg_ref[h]
            intra_ref[h] = jnp.where(rel >= 0, jnp.exp(lg * jnp.maximum(rel, 0.0)), 0.0)
            qdec_ref[h] = jnp.exp(lg * (rowf + 1.0))
            kdec_ref[h] = jnp.exp(lg * (L - 1.0 - rowf))
        ph = lax.broadcasted_iota(jnp.int32, (L, half), 0).astype(F32) * ang_ref[...]
        cos_ref[...] = jnp.cos(ph)
        sin_ref[...] = jnp.sin(ph)

    @pl.when(c > 0)
    def _():
        step = float(L) * ang_ref[...]
        cd, sd = jnp.cos(step), jnp.sin(step)
        c0, s0 = cos_ref[...], sin_ref[...]
        cos_ref[...] = c0 * cd - s0 * sd
        sin_ref[...] = s0 * cd + c0 * sd

    cos, sin = cos_ref[...], sin_ref[...]

    def rot(x):
        x0, x1 = x[:, :half], x[:, half:]
        return jnp.concatenate([x0 * cos - x1 * sin, x1 * cos + x0 * sin], axis=1)

    qk = lambda ref, h: rot(ref[:, h * RET_QK_DIM:(h + 1) * RET_QK_DIM].astype(F32))
    vcols = [slice(h * RET_V_DIM, (h + 1) * RET_V_DIM) for h in heads]
    qbs = [qk(q_ref, h).astype(BF16) for h in heads]
    ks = [qk(k_ref, h) * (RET_QK_DIM ** -0.5) for h in heads]
    vbs = [v_ref[:, vcols[h]].astype(BF16) for h in heads]
    scores = [(_dot_nt(qbs[h], ks[h].astype(BF16)) * intra_ref[h]).astype(BF16) for h in heads]
    inters = [_dot(qbs[h], st_ref[h].astype(BF16)) for h in heads]
    for h in heads:
        kd = ks[h] * jnp.concatenate([kdec_ref[h]] * (RET_QK_DIM // LANES), axis=1)
        st_ref[h] = st_ref[h] * jnp.exp(lg_ref[h][:, 0:1] * float(L)) + _dot_tn(kd.astype(BF16), vbs[h])
    for h in heads:
        o = _dot(scores[h], vbs[h]) + jnp.concatenate([qdec_ref[h]] * (RET_V_DIM // LANES), axis=1) * inters[h]
        ms = jnp.mean(o * o, axis=-1, keepdims=True)
        o_ref[:, vcols[h]] = (o * lax.rsqrt(ms + EPS) * _silu(g_ref[:, vcols[h]].astype(F32))).astype(o_ref.dtype)


def _ret_core(proj):
    s = proj.shape[0]
    L = RET_CHUNK
    nh = RET_HEADS
    half = RET_QK_DIM // 2
    angle = (1.0 / (RET_ROT_BASE ** jnp.linspace(0.0, 1.0, half, dtype=F32))).reshape(1, half)
    log_g = jnp.log1p(-jnp.exp2(-5.0 - jnp.arange(nh, dtype=F32)))
    lg = jnp.broadcast_to(log_g[:, None, None], (nh, 1, LANES))
    return pl.pallas_call(
        _ret_kernel,
        out_shape=jax.ShapeDtypeStruct((s, RET_V_WIDTH), BF16),
        grid=(s // L,),
        in_specs=[pl.BlockSpec((L, D_MODEL), lambda c: (c, 0)),
                  pl.BlockSpec((L, D_MODEL), lambda c: (c, 1)),
                  pl.BlockSpec((L, RET_V_WIDTH), lambda c: (c, 1)),
                  pl.BlockSpec((L, RET_V_WIDTH), lambda c: (c, 2)),
                  pl.BlockSpec((1, half), lambda c: (0, 0)),
                  pl.BlockSpec((nh, 1, LANES), lambda c: (0, 0, 0))],
        out_specs=pl.BlockSpec((L, RET_V_WIDTH), lambda c: (c, 0)),
        scratch_shapes=[pltpu.VMEM((nh, RET_QK_DIM, RET_V_DIM), F32),
                        pltpu.VMEM((nh, L, L), F32), pltpu.VMEM((nh, L, LANES), F32),
                        pltpu.VMEM((nh, L, LANES), F32),
                        pltpu.VMEM((L, half), F32), pltpu.VMEM((L, half), F32)],
        compiler_params=_params("arbitrary"),
        name="retention",
    )(proj, proj, proj, proj, angle, lg)


def _ret_permute_kernel(w_ref, o_ref):
    src = lax.broadcasted_iota(jnp.int32, (RET_QK_DIM, RET_QK_DIM), 0)
    dst = lax.broadcasted_iota(jnp.int32, (RET_QK_DIM, RET_QK_DIM), 1)
    half = RET_QK_DIM // 2
    perm = jnp.where(src == 2 * (dst % half) + dst // half, 1.0, 0.0).astype(BF16)
    o_ref[...] = _dot(w_ref[...], perm).astype(o_ref.dtype)


def _ret_permute_qk(w):
    d, n = w.shape
    blk = pl.BlockSpec((d, RET_QK_DIM), lambda hb: (0, hb))
    return pl.pallas_call(
        _ret_permute_kernel,
        out_shape=jax.ShapeDtypeStruct((d, n), w.dtype),
        grid=(2 * RET_HEADS,),
        in_specs=[blk],
        out_specs=blk,
        input_output_aliases={0: 0},
        compiler_params=_params("arbitrary"),
        name="ret_permute_qk",
    )(w)


def kernel(x, norm_g, sb_w_in, sb_w_out, hg_w_in, hg_lb_logits, hg_norm_g, hg_w_out, m2_w_in, m2_conv_w,
           m2_conv_b, m2_dt_bias, m2_a_log, m2_d, m2_norm_g, m2_w_out, ret_w_in, ret_w_out, final_g):
    b, s, d = x.shape
    depth = norm_g.shape[0]
    lb_cum = jnp.cumsum(jax.nn.softmax(hg_lb_logits.astype(F32), axis=0), axis=0)
    lower_bounds = lb_cum - lb_cum[0]
    outs = []
    for bi in range(b):
        xb = x[bi]
        for i in range(depth):
            m, j = i % 4, i // 4
            fg = final_g if i == depth - 1 else None
            if m == 0:
                proj = _norm_proj(xb, norm_g[i], sb_w_in[j].astype(BF16))
                xb = _out_proj(_sb_core(proj), sb_w_out[j], xb, fg)
            elif m == 1:
                proj = _norm_proj(xb, norm_g[i], hg_w_in[j].astype(BF16))
                og = _hgrn_core(proj, lower_bounds[i], hg_norm_g[j])
                xb = _out_proj(og, hg_w_out[j], xb, fg)
            elif m == 2:
                n_main = M2_WIDTH + M2_WIDTH + 2 * M2_GROUPS * M2_STATE
                w_all = m2_w_in[j].astype(BF16)
                w_dt = jnp.pad(w_all[:, n_main:], ((0, 0), (0, LANES - M2_HEADS)))
                proj, dt_raw = _m2_in_proj(xb, norm_g[i], w_all, w_dt, n_main)
                og = _m2_core(proj, dt_raw, m2_conv_w[j], m2_conv_b[j], m2_dt_bias[j], m2_a_log[j], m2_d[j],
                              m2_norm_g[j])
                xb = _out_proj(og, m2_w_out[j], xb, fg)
            else:
                w_in = _ret_permute_qk(ret_w_in[j].astype(BF16))
                proj = _norm_proj(xb, norm_g[i], w_in)
                xb = _out_proj(_ret_core(proj), ret_w_out[j], xb, fg)
        outs.append(xb)
    return outs[0][None] if b == 1 else jnp.stack(outs, axis=0)
```
